```python
import math
import jax, jax.numpy as jnp
from jax import lax
import numpy as np

D_MODEL = 1024
BATCH = 1
SEQ = 16384
DEPTH = 2

A_GROUPS = 8
A_GROUP_DIM = 64
A_CHUNK = 128
B_HEADS = 8
B_HEAD_DIM = 64
B_KV_GROUPS = 2
B_HPG = B_HEADS // B_KV_GROUPS
CMP_BLOCK = 32
CMP_STRIDE = 16
CMP_HIDDEN = 128
SLC_BLOCK = 64
SLC_TOP_N = 16
WINDOW = 512
Q_BLOCK = 128
FORCED_SCORE = 1e4
REL_BUCKETS = 32
REL_MAX_DIST = 2048
CONV_WIDTH = 31
FFN_DIM = 2816
FFN_CONV_WIDTH = 3
LN_EPS = 1e-5
ALPHA = (2 * DEPTH) ** 0.25
BETA = (8 * DEPTH) ** -0.25
N_AB = (DEPTH + 1) // 2
N_C = DEPTH // 2

A_WIDTH = A_GROUPS * A_GROUP_DIM
B_WIDTH = B_HEADS * B_HEAD_DIM
KV_WIDTH = B_KV_GROUPS * B_HEAD_DIM
AB_IN_WIDTH = 2 * A_WIDTH + B_WIDTH + 6 * KV_WIDTH + 3 * B_HEADS
MIX_WIDTH = A_WIDTH + B_WIDTH

kernel_name = 'hybrid_gmlp_nsa_conformer_convffn_deepnorm'


def layer_norm(x, g, b):
    xf = x.astype(jnp.float32)
    mu = jnp.mean(xf, axis=-1, keepdims=True)
    var = jnp.mean(jnp.square(xf - mu), axis=-1, keepdims=True)
    y = (xf - mu) * lax.rsqrt(var + LN_EPS)
    return (y * g.astype(jnp.float32) + b.astype(jnp.float32)).astype(x.dtype)


def causal_dwconv(x, w, b):
    width, ch = w.shape
    xp = jnp.pad(x, ((0, 0), (width - 1, 0), (0, 0)))
    y = lax.conv_general_dilated(xp, w[:, None, :], window_strides=(1,), padding='VALID',
                                 dimension_numbers=('NWC', 'WIO', 'NWC'), feature_group_count=ch)
    return y + b


def rel_bucket(dist):
    n = jnp.maximum(dist, 0)
    max_exact = REL_BUCKETS // 2
    nf = jnp.maximum(n, max_exact).astype(jnp.float32)
    large = max_exact + (jnp.log(nf / max_exact) / math.log(REL_MAX_DIST / max_exact)
                         * (REL_BUCKETS - max_exact)).astype(jnp.int32)
    large = jnp.minimum(large, REL_BUCKETS - 1)
    return jnp.where(n < max_exact, n, large)


def masked_softmax(logits, mask):
    l = jnp.where(mask, logits.astype(jnp.float32), -1e30)
    m = jnp.max(l, axis=-1, keepdims=True)
    e = jnp.where(mask, jnp.exp(l - m), 0.0)
    return e / jnp.maximum(jnp.sum(e, axis=-1, keepdims=True), 1e-30)


def compress_kv(kv, pe, w1, w2):
    bsz, t_len = kv.shape[0], kv.shape[1]
    ratio = CMP_BLOCK // CMP_STRIDE
    n_cmp = t_len // CMP_STRIDE - ratio + 1
    chunks = kv.reshape(bsz, t_len // CMP_STRIDE, CMP_STRIDE, B_KV_GROUPS, B_HEAD_DIM)
    blocks = jnp.concatenate([chunks[:, r:r + n_cmp] for r in range(ratio)], axis=2)
    blocks = blocks + pe[None, None, :, None, :]
    flat = blocks.transpose(0, 1, 3, 2, 4).reshape(bsz, n_cmp, B_KV_GROUPS, CMP_BLOCK * B_HEAD_DIM)
    return jax.nn.gelu(flat @ w1) @ w2


def nsa_core(q, kc, vc, ks, vs, kw, vw, gates, rel_table):
    t_len = q.shape[0]
    G, P, dh = B_KV_GROUPS, B_HPG, B_HEAD_DIM
    q = (q * dh ** -0.5).reshape(t_len, G, P, dh)
    gates = gates.reshape(t_len, G, P, 3)
    n_cmp = kc.shape[0]
    n_slc = t_len // SLC_BLOCK
    top_n = min(SLC_TOP_N, n_slc)
    ks_blk = ks.reshape(n_slc, SLC_BLOCK, G, dh).transpose(2, 0, 1, 3)
    vs_blk = vs.reshape(n_slc, SLC_BLOCK, G, dh).transpose(2, 0, 1, 3)
    kw_pad = jnp.pad(kw.reshape(t_len, G, dh), ((WINDOW, 0), (0, 0), (0, 0)))
    vw_pad = jnp.pad(vw.reshape(t_len, G, dh), ((WINDOW, 0), (0, 0), (0, 0)))
    tbl = rel_table.reshape(REL_BUCKETS, G, P)
    cmp_end = jnp.arange(n_cmp) * CMP_STRIDE + CMP_BLOCK - 1
    cmp_start = cmp_end - (CMP_BLOCK - 1)
    slc_start = jnp.arange(n_slc) * SLC_BLOCK
    overlap = ((cmp_end[:, None] >= slc_start[None, :]) &
               (cmp_start[:, None] <= slc_start[None, :] + SLC_BLOCK - 1)).astype(jnp.float32)
    g_idx = jnp.arange(G)[None, :, None]
    tok = jnp.arange(SLC_BLOCK)
    win_off = jnp.arange(Q_BLOCK + WINDOW) - WINDOW

    def bias_full(dist):
        return tbl[rel_bucket(dist)].transpose(0, 2, 3, 1)

    def query_block(qb):
        s0 = qb * Q_BLOCK
        t = s0 + jnp.arange(Q_BLOCK)
        qq = lax.dynamic_slice_in_dim(q, s0, Q_BLOCK, 0)
        gg = jax.nn.sigmoid(lax.dynamic_slice_in_dim(gates, s0, Q_BLOCK, 0).astype(jnp.float32))
        dist_c = t[:, None] - cmp_end[None, :]
        lc = jnp.einsum('qgpd,cgd->qgpc', qq, kc).astype(jnp.float32) + bias_full(dist_c)
        pc = masked_softmax(lc, (dist_c >= 0)[:, None, None, :])
        oc = jnp.einsum('qgpc,cgd->qgpd', pc.astype(vc.dtype), vc)
        score = jnp.einsum('qgc,cs->qgs', jnp.sum(pc, axis=2), overlap)
        blk_valid = slc_start[None, :] <= t[:, None]
        forced = (slc_start[None, :] == ((t // SLC_BLOCK) * SLC_BLOCK)[:, None]) | (slc_start[None, :] == 0)
        score = jnp.where(forced[:, None, :], FORCED_SCORE, jnp.where(blk_valid[:, None, :], score, -1.0))
        _, idx = lax.top_k(score, top_n)
        ksel = ks_blk[g_idx, idx].reshape(Q_BLOCK, G, top_n * SLC_BLOCK, dh)
        vsel = vs_blk[g_idx, idx].reshape(Q_BLOCK, G, top_n * SLC_BLOCK, dh)
        pos_s = (idx[..., None] * SLC_BLOCK + tok).reshape(Q_BLOCK, G, top_n * SLC_BLOCK)
        dist_s = t[:, None, None] - pos_s
        bias_s = tbl[rel_bucket(dist_s), g_idx].transpose(0, 1, 3, 2)
        ls = jnp.einsum('qgpd,qgkd->qgpk', qq, ksel).astype(jnp.float32) + bias_s
        ps = masked_softmax(ls, (dist_s >= 0)[:, :, None, :])
        osl = jnp.einsum('qgpk,qgkd->qgpd', ps.astype(vsel.dtype), vsel)
        kwb = lax.dynamic_slice_in_dim(kw_pad, s0, Q_BLOCK + WINDOW, 0)
        vwb = lax.dynamic_slice_in_dim(vw_pad, s0, Q_BLOCK + WINDOW, 0)
        pos_w = s0 + win_off
        dist_w = t[:, None] - pos_w[None, :]
        mask_w = (dist_w >= 0) & (dist_w < WINDOW) & (pos_w[None, :] >= 0)
        lw = jnp.einsum('qgpd,kgd->qgpk', qq, kwb).astype(jnp.float32) + bias_full(dist_w)
        pw = masked_softmax(lw, mask_w[:, None, None, :])
        ow = jnp.einsum('qgpk,kgd->qgpd', pw.astype(vwb.dtype), vwb)
        out = gg[..., 0:1] * oc + gg[..., 1:2] * osl + gg[..., 2:3] * ow
        return out.reshape(Q_BLOCK, B_WIDTH).astype(q.dtype)

    out = lax.map(query_block, jnp.arange(t_len // Q_BLOCK))
    return out.reshape(t_len, B_WIDTH)


def mixer_ab(x, rel_table, w_in, sgu_ln_g, sgu_ln_b, sgu_w, sgu_b,
             pe_k, w1_k, w2_k, pe_v, w1_v, w2_v, w_out):
    bsz, t_len, _ = x.shape
    h = x @ w_in
    splits = [int(s) for s in np.cumsum([A_WIDTH, A_WIDTH, B_WIDTH] + [KV_WIDTH] * 6)]
    u, v, q, kc, vc, ks, vs, kw, vw, gates = jnp.split(h, splits, axis=-1)
    u = jax.nn.gelu(u)
    v = layer_norm(jax.nn.gelu(v), sgu_ln_g, sgu_ln_b)
    v = v.reshape(bsz, t_len // A_CHUNK, A_CHUNK, A_GROUPS, A_GROUP_DIM)
    causal = jnp.tril(jnp.ones((A_CHUNK, A_CHUNK), dtype=bool))
    w_s = jnp.where(causal[None], sgu_w, 0.0)
    s = jnp.einsum('gij,bcjgd->bcigd', w_s, v) + sgu_b.T[None, None, :, :, None]
    a_out = u * s.reshape(bsz, t_len, A_WIDTH)
    kv_shape = (bsz, t_len, B_KV_GROUPS, B_HEAD_DIM)
    kcc = compress_kv(kc.reshape(kv_shape), pe_k, w1_k, w2_k)
    vcc = compress_kv(vc.reshape(kv_shape), pe_v, w1_v, w2_v)
    b_out = jax.vmap(nsa_core, in_axes=(0, 0, 0, 0, 0, 0, 0, 0, None))(
        q, kcc, vcc, ks, vs, kw, vw, gates, rel_table)
    return jnp.concatenate([a_out, b_out], axis=-1) @ w_out


def mixer_c(x, w_in, b_in, dw_w, dw_b, norm_g, norm_b, w_out):
    h = x @ w_in + b_in
    a, gt = jnp.split(h, 2, axis=-1)
    h = a * jax.nn.sigmoid(gt)
    h = causal_dwconv(h, dw_w, dw_b)
    h = jax.nn.silu(layer_norm(h, norm_g, norm_b))
    return h @ w_out


def conv_ffn(x, w_up, conv_w, conv_b, w_down):
    h = causal_dwconv(x @ w_up, conv_w, conv_b)
    g, u = jnp.split(h, 2, axis=-1)
    return (jax.nn.gelu(g) * u) @ w_down


def setup_inputs(seed: int = 0) -> dict:
    key = jax.random.key(seed)
    keys = iter(jax.random.split(key, 64))

    def nrm(shape, scale):
        return jax.random.normal(next(keys), shape, jnp.float32) * scale

    def gain(shape):
        return 1.0 + nrm(shape, 0.02)

    D, F = D_MODEL, FFN_DIM
    cmp_in = CMP_BLOCK * B_HEAD_DIM
    return {
        'x': nrm((BATCH, SEQ, D), 1.0),
        'rel_table': nrm((REL_BUCKETS, B_HEADS), 0.5),
        'ab_w_in': nrm((N_AB, D, AB_IN_WIDTH), D ** -0.5),
        'ab_sgu_ln_g': gain((N_AB, A_WIDTH)),
        'ab_sgu_ln_b': nrm((N_AB, A_WIDTH), 0.02),
        'ab_sgu_w': nrm((N_AB, A_GROUPS, A_CHUNK, A_CHUNK), 0.5 * A_CHUNK ** -0.5),
        'ab_sgu_b': 1.0 + nrm((N_AB, A_GROUPS, A_CHUNK), 0.1),
        'ab_cmp_pe_k': nrm((N_AB, CMP_BLOCK, B_HEAD_DIM), 0.1),
        'ab_cmp_w1_k': nrm((N_AB, cmp_in, CMP_HIDDEN), cmp_in ** -0.5),
        'ab_cmp_w2_k': nrm((N_AB, CMP_HIDDEN, B_HEAD_DIM), CMP_HIDDEN ** -0.5),
        'ab_cmp_pe_v': nrm((N_AB, CMP_BLOCK, B_HEAD_DIM), 0.1),
        'ab_cmp_w1_v': nrm((N_AB, cmp_in, CMP_HIDDEN), cmp_in ** -0.5),
        'ab_cmp_w2_v': nrm((N_AB, CMP_HIDDEN, B_HEAD_DIM), CMP_HIDDEN ** -0.5),
        'ab_w_out': nrm((N_AB, MIX_WIDTH, D), BETA * MIX_WIDTH ** -0.5),
        'c_w_in': nrm((N_C, D, 2 * D), D ** -0.5),
        'c_b_in': nrm((N_C, 2 * D), 0.01),
        'c_dw_w': nrm((N_C, CONV_WIDTH, D), CONV_WIDTH ** -0.5),
        'c_dw_b': nrm((N_C, D), 0.01),
        'c_norm_g': gain((N_C, D)),
        'c_norm_b': nrm((N_C, D), 0.02),
        'c_w_out': nrm((N_C, D, D), BETA * D ** -0.5),
        'ffn_w_up': nrm((DEPTH, D, 2 * F), D ** -0.5),
        'ffn_conv_w': nrm((DEPTH, FFN_CONV_WIDTH, 2 * F), FFN_CONV_WIDTH ** -0.5),
        'ffn_conv_b': nrm((DEPTH, 2 * F), 0.01),
        'ffn_w_down': nrm((DEPTH, F, D), BETA * F ** -0.5),
        'ln_mix_g': gain((DEPTH, D)),
        'ln_mix_b': nrm((DEPTH, D), 0.02),
        'ln_ffn_g': gain((DEPTH, D)),
        'ln_ffn_b': nrm((DEPTH, D), 0.02),
    }


def reference(x, rel_table, ab_w_in, ab_sgu_ln_g, ab_sgu_ln_b, ab_sgu_w, ab_sgu_b,
              ab_cmp_pe_k, ab_cmp_w1_k, ab_cmp_w2_k, ab_cmp_pe_v, ab_cmp_w1_v, ab_cmp_w2_v, ab_w_out,
              c_w_in, c_b_in, c_dw_w, c_dw_b, c_norm_g, c_norm_b, c_w_out,
              ffn_w_up, ffn_conv_w, ffn_conv_b, ffn_w_down,
              ln_mix_g, ln_mix_b, ln_ffn_g, ln_ffn_b):
    for layer in range(DEPTH):
        i = layer // 2
        if layer % 2 == 0:
            mix = mixer_ab(x, rel_table, ab_w_in[i], ab_sgu_ln_g[i], ab_sgu_ln_b[i], ab_sgu_w[i], ab_sgu_b[i],
                           ab_cmp_pe_k[i], ab_cmp_w1_k[i], ab_cmp_w2_k[i],
                           ab_cmp_pe_v[i], ab_cmp_w1_v[i], ab_cmp_w2_v[i], ab_w_out[i])
        else:
            mix = mixer_c(x, c_w_in[i], c_b_in[i], c_dw_w[i], c_dw_b[i], c_norm_g[i], c_norm_b[i], c_w_out[i])
        x = layer_norm(ALPHA * x + mix, ln_mix_g[layer], ln_mix_b[layer])
        ffn = conv_ffn(x, ffn_w_up[layer], ffn_conv_w[layer], ffn_conv_b[layer], ffn_w_down[layer])
        x = layer_norm(ALPHA * x + ffn, ln_ffn_g[layer], ln_ffn_b[layer])
    return x
```

```python
import functools
import math

import numpy as np
import jax
import jax.numpy as jnp
from jax import lax
from jax.experimental import pallas as pl
from jax.experimental.pallas import tpu as pltpu

DEPTH = 2
A_GROUPS = 8
A_GROUP_DIM = 64
A_CHUNK = 128
B_HEADS = 8
B_HEAD_DIM = 64
B_KV_GROUPS = 2
B_HPG = B_HEADS // B_KV_GROUPS
CMP_BLOCK = 32
CMP_STRIDE = 16
SLC_BLOCK = 64
SLC_TOP_N = 16
WINDOW = 512
Q_BLOCK = 128
FORCED_SCORE = 1e4
REL_BUCKETS = 32
REL_MAX_DIST = 2048
LN_EPS = 1e-5
ALPHA = (2 * DEPTH) ** 0.25

A_WIDTH = A_GROUPS * A_GROUP_DIM
B_WIDTH = B_HEADS * B_HEAD_DIM
KV_WIDTH = B_KV_GROUPS * B_HEAD_DIM

LANES = 128
SUBLANES = 8
VMEM_LIMIT = 56 * 1024 * 1024

NEG = -1e30
M_INIT = -5e29
QL = B_HPG * Q_BLOCK
NL = B_KV_GROUPS * QL
KEY_TILE = 512
SLC_PER_TILE = KEY_TILE // SLC_BLOCK
FAR_DIST = 1512
NEAR_BACK = 2048
NEAR_ROWS = NEAR_BACK + KEY_TILE
WIN_ROWS = WINDOW + Q_BLOCK
CMP_WIN = 128
CMP_SPECIAL = 16


def _gelu(x):
    c = math.sqrt(2.0 / math.pi)
    return x * (0.5 * (1.0 + jnp.tanh(c * (x + 0.044715 * (x * x * x)))))


def _layer_norm(x, g, b):
    mu = jnp.mean(x, axis=-1, keepdims=True)
    xc = x - mu
    var = jnp.mean(xc * xc, axis=-1, keepdims=True)
    return xc * lax.rsqrt(var + LN_EPS) * g + b


def _sigmoid(x):
    return 1.0 / (1.0 + jnp.exp(-x))


def _const_spec(shape):
    n = len(shape)
    return pl.BlockSpec(shape, lambda *_: (0,) * n, pipeline_mode=pl.Buffered(1))


def _params(sem):
    return pltpu.CompilerParams(dimension_semantics=sem, vmem_limit_bytes=VMEM_LIMIT)


def _proj_kernel(x_ref, w_ref, uv_ref, q_ref, cmp_ref, kv_ref, gate_ref):
    xb = x_ref[...].astype(jnp.bfloat16)
    o = 0
    for ref, scale in ((uv_ref, None), (q_ref, B_HEAD_DIM ** -0.5), (cmp_ref, None), (kv_ref, None),
                       (gate_ref, None)):
        n = ref.shape[1]
        h = jnp.dot(xb, w_ref[:, o:o + n], preferred_element_type=jnp.float32)
        if scale is not None:
            h = h * scale
        ref[...] = h.astype(ref.dtype)
        o += n


def _proj(x, w, tm=512):
    t, d = x.shape
    widths = (2 * A_WIDTH, B_HEADS * LANES, 2 * KV_WIDTH, 4 * KV_WIDTH, LANES)
    dtypes = (jnp.float32, jnp.bfloat16, jnp.float32, jnp.bfloat16, jnp.float32)
    assert w.shape == (d, sum(widths)) and t % tm == 0
    return pl.pallas_call(
        _proj_kernel,
        grid=(t // tm,),
        in_specs=[pl.BlockSpec((tm, d), lambda i: (i, 0)), _const_spec(w.shape)],
        out_specs=[pl.BlockSpec((tm, n), lambda i: (i, 0)) for n in widths],
        out_shape=[jax.ShapeDtypeStruct((t, n), dt) for n, dt in zip(widths, dtypes)],
        compiler_params=_params(("parallel",)),
        name="ab_in_proj",
    )(x, w)


def _sgu_kernel(u_ref, v_ref, g_ref, b_ref, ws_ref, bias_ref, o_ref):
    tm = u_ref.shape[0]
    u = _gelu(u_ref[...])
    v = _layer_norm(_gelu(v_ref[...]), g_ref[...], b_ref[...]).astype(jnp.bfloat16)
    lane_group = lax.broadcasted_iota(jnp.int32, (A_CHUNK, A_WIDTH), 1) // A_GROUP_DIM
    for c in range(tm // A_CHUNK):
        rows = slice(c * A_CHUNK, (c + 1) * A_CHUNK)
        vc = v[rows]
        s = bias_ref[...]
        for g in range(A_GROUPS):
            sg = jnp.dot(ws_ref[g], vc, preferred_element_type=jnp.float32)
            s = s + jnp.where(lane_group == g, sg, 0.0)
        o_ref[rows, :] = (u[rows] * s).astype(o_ref.dtype)


def _sgu(uv, ln_g, ln_b, ws, bias, tm=512):
    t = uv.shape[0]
    return pl.pallas_call(
        _sgu_kernel,
        grid=(t // tm,),
        in_specs=[pl.BlockSpec((tm, A_WIDTH), lambda i: (i, 0)),
                  pl.BlockSpec((tm, A_WIDTH), lambda i: (i, 1)),
                  _const_spec(ln_g.shape), _const_spec(ln_b.shape),
                  _const_spec(ws.shape), _const_spec(bias.shape)],
        out_specs=pl.BlockSpec((tm, A_WIDTH), lambda i: (i, 0)),
        out_shape=jax.ShapeDtypeStruct((t, A_WIDTH), jnp.bfloat16),
        compiler_params=_params(("parallel",)),
        name="sgu_mixer",
    )(uv, uv, ln_g, ln_b, ws, bias)


def _compress_kernel(r_ref, nxt_ref, pea_ref, peb_ref, wa_ref, wb_ref, w2_ref, o_ref):
    tm = r_ref.shape[0]
    r = r_ref[...]
    xa = (r + pea_ref[...]).astype(jnp.bfloat16)
    xb = (jnp.concatenate([r, nxt_ref[...]], axis=0) + peb_ref[...]).astype(jnp.bfloat16)
    ha = jnp.dot(xa, wa_ref[...], preferred_element_type=jnp.float32)
    hb = jnp.dot(xb, wb_ref[...], preferred_element_type=jnp.float32)
    hid = _gelu(ha + hb[1:tm + 1]).astype(jnp.bfloat16)
    o_ref[...] = jnp.dot(hid, w2_ref[...], preferred_element_type=jnp.float32)


def _compress(r, pea, peb, wa, wb, w2, tm=256):
    n, width = r.shape
    nblk = n // SUBLANES
    return pl.pallas_call(
        _compress_kernel,
        grid=(n // tm,),
        in_specs=[pl.BlockSpec((tm, width), lambda i: (i, 0)),
                  pl.BlockSpec((SUBLANES, width),
                               lambda i: (jnp.minimum((i + 1) * (tm // SUBLANES), nblk - 1), 0)),
                  _const_spec(pea.shape), _const_spec(peb.shape),
                  _const_spec(wa.shape), _const_spec(wb.shape), _const_spec(w2.shape)],
        out_specs=pl.BlockSpec((tm, w2.shape[1]), lambda i: (i, 0)),
        out_shape=jax.ShapeDtypeStruct((n, w2.shape[1]), jnp.float32),
        compiler_params=_params(("parallel",)),
        name="kv_compress",
    )(r, r, pea, peb, wa, wb, w2)


def _softmax_cols(l):
    m = jnp.maximum(jnp.max(l, axis=0, keepdims=True), M_INIT)
    e = jnp.exp(l - m)
    return e, jnp.sum(e, axis=0, keepdims=True)


def _nsa_kernel(q_ref, gate_ref, kc_ref, vct_ref, ks_ref, vst_ref, kw_ref, vwt_ref,
                bcmp_ref, bnear_ref, bwin_ref, o_ref,
                sc_ref, pcs_ref, sel_ref, acc_ref):
    qb = pl.program_id(0)
    s0 = qb * Q_BLOCK
    n_cmp = kc_ref.shape[0]
    n_slc = sel_ref.shape[1]

    q_all = jnp.concatenate([q_ref[:, h * LANES:(h + 1) * LANES] for h in range(B_HEADS)], axis=0)

    def scores(k):
        return lax.dot_general(k, q_all, (((1,), (1,)), ((), ())), preferred_element_type=jnp.float32)

    c0 = jnp.maximum(qb * (Q_BLOCK // CMP_STRIDE) - (CMP_WIN - Q_BLOCK // CMP_STRIDE), 0)
    c0 = pl.multiple_of(c0, SUBLANES)
    row_c = lax.broadcasted_iota(jnp.int32, (n_cmp, 1), 0)
    sc_ref[...] = jnp.where(row_c < c0 + CMP_WIN, scores(kc_ref[...]), NEG)
    sc_ref[pl.ds(c0, CMP_WIN), :] += bcmp_ref[0]
    e_c, sum_c = _softmax_cols(sc_ref[...])
    inv_c = 1.0 / jnp.maximum(sum_c, 1e-30)
    oc = jnp.dot(vct_ref[...], e_c.astype(jnp.bfloat16), preferred_element_type=jnp.float32)

    pc = e_c * inv_c
    t_row = s0 + lax.broadcasted_iota(jnp.int32, (1, Q_BLOCK), 1)
    blk = lax.broadcasted_iota(jnp.int32, (n_slc, Q_BLOCK), 0)
    blk_f = blk.astype(jnp.float32)
    own = blk == t_row // SLC_BLOCK
    valid = blk * SLC_BLOCK <= t_row
    ratio = SLC_BLOCK // CMP_STRIDE
    pcs_ref[0:SUBLANES, :] = jnp.zeros((SUBLANES, Q_BLOCK), jnp.float32)
    for g in range(B_KV_GROUPS):
        pg = pc[:, g * QL:(g + 1) * QL]
        pcs_ref[SUBLANES:, :] = sum(pg[:, p * Q_BLOCK:(p + 1) * Q_BLOCK] for p in range(B_HPG))
        score = pcs_ref[pl.ds(SUBLANES - 1, n_slc, stride=ratio), :]
        for r in range(ratio):
            score = score + pcs_ref[pl.ds(SUBLANES + r, n_slc, stride=ratio), :]
        score = jnp.where(valid, score, -1.0)
        score = jnp.where(own, FORCED_SCORE, jnp.where(blk == 0, FORCED_SCORE, score))
        sel = jnp.full((n_slc, Q_BLOCK), NEG, jnp.float32)
        for _ in range(min(SLC_TOP_N, n_slc)):
            best = jnp.max(score, axis=0, keepdims=True)
            first = jnp.min(jnp.where(score == best, blk_f, float(n_slc)), axis=0, keepdims=True)
            hit = blk_f == first
            sel = jnp.where(hit, 0.0, sel)
            score = jnp.where(hit, -2.0, score)
        sel_ref[g] = sel

    def sel_mask(m):
        parts = []
        for g in range(B_KV_GROUPS):
            rows = sel_ref[g, pl.ds(pl.multiple_of(m * SLC_PER_TILE, SUBLANES), SLC_PER_TILE), :]
            mg = jnp.concatenate([jnp.broadcast_to(rows[b:b + 1, :], (SLC_BLOCK, Q_BLOCK))
                                  for b in range(SLC_PER_TILE)], axis=0)
            parts.extend([mg] * B_HPG)
        return jnp.concatenate(parts, axis=1)

    def tile_step(m, carry, near):
        m_run, s_run = carry
        k0 = pl.multiple_of(m * KEY_TILE, KEY_TILE)
        l = scores(ks_ref[pl.ds(k0, KEY_TILE), :]) + sel_mask(m)
        if near:
            r0 = pl.multiple_of(k0 - s0 + NEAR_BACK, Q_BLOCK)
            l = l + bnear_ref[pl.ds(r0, KEY_TILE), :]
        m_new = jnp.maximum(m_run, jnp.max(l, axis=0, keepdims=True))
        alpha = jnp.exp(m_run - m_new)
        e = jnp.exp(l - m_new)
        s_new = alpha * s_run + jnp.sum(e, axis=0, keepdims=True)
        pv = jnp.dot(vst_ref[:, pl.ds(k0, KEY_TILE)], e.astype(jnp.bfloat16),
                     preferred_element_type=jnp.float32)
        acc_ref[...] = alpha * acc_ref[...] + pv
        return m_new, s_new

    m_far = jnp.maximum((s0 - (FAR_DIST - 1)) // KEY_TILE, 0)
    m_end = s0 // KEY_TILE + 1
    acc_ref[...] = jnp.zeros_like(acc_ref)
    carry = (jnp.full((1, NL), M_INIT, jnp.float32), jnp.zeros((1, NL), jnp.float32))
    carry = lax.fori_loop(0, m_far, functools.partial(tile_step, near=False), carry)
    _, sum_s = lax.fori_loop(m_far, m_end, functools.partial(tile_step, near=True), carry)
    osl = acc_ref[...]
    inv_s = 1.0 / jnp.maximum(sum_s, 1e-30)

    k0w = pl.multiple_of(s0, Q_BLOCK)
    row_w = lax.broadcasted_iota(jnp.int32, (WIN_ROWS, 1), 0)
    lw = scores(kw_ref[pl.ds(k0w, WIN_ROWS), :]) + bwin_ref[...]
    lw = jnp.where(row_w >= WINDOW - s0, lw, NEG)
    e_w, sum_w = _softmax_cols(lw)
    inv_w = 1.0 / jnp.maximum(sum_w, 1e-30)
    ow = jnp.dot(vwt_ref[:, pl.ds(k0w, WIN_ROWS)], e_w.astype(jnp.bfloat16),
                 preferred_element_type=jnp.float32)

    gt = _sigmoid(gate_ref[...]).T

    def gate_row(k):
        return jnp.concatenate([gt[k * B_HEADS + h:k * B_HEADS + h + 1, :] for h in range(B_HEADS)], axis=1)

    out_t = (oc * (gate_row(0) * inv_c) + osl * (gate_row(1) * inv_s) + ow * (gate_row(2) * inv_w))
    heads = []
    for h in range(B_HEADS):
        g = h // B_HPG
        blk_t = out_t[g * B_HEAD_DIM:(g + 1) * B_HEAD_DIM, h * Q_BLOCK:(h + 1) * Q_BLOCK]
        heads.append(blk_t.T)
    o_ref[...] = jnp.concatenate(heads, axis=1).astype(o_ref.dtype)


def _nsa(q, gates, kc, vct, ks, vst, kw, vwt, bcmp, bnear, bwin):
    t = q.shape[0]
    n_cmp = kc.shape[0]
    n_slc = t // SLC_BLOCK
    return pl.pallas_call(
        _nsa_kernel,
        grid=(t // Q_BLOCK,),
        in_specs=[pl.BlockSpec((Q_BLOCK, B_HEADS * LANES), lambda i: (i, 0)),
                  pl.BlockSpec((Q_BLOCK, LANES), lambda i: (i, 0)),
                  _const_spec(kc.shape), _const_spec(vct.shape),
                  _const_spec(ks.shape), _const_spec(vst.shape),
                  _const_spec(kw.shape), _const_spec(vwt.shape),
                  pl.BlockSpec((1, CMP_WIN, NL), lambda i: (jnp.minimum(i, CMP_SPECIAL - 1), 0, 0)),
                  _const_spec(bnear.shape), _const_spec(bwin.shape)],
        out_specs=pl.BlockSpec((Q_BLOCK, B_WIDTH), lambda i: (i, 0)),
        out_shape=jax.ShapeDtypeStruct((t, B_WIDTH), jnp.bfloat16),
        scratch_shapes=[pltpu.VMEM((n_cmp, NL), jnp.float32),
                        pltpu.VMEM((n_cmp + SUBLANES, Q_BLOCK), jnp.float32),
                        pltpu.VMEM((B_KV_GROUPS, n_slc, Q_BLOCK), jnp.float32),
                        pltpu.VMEM((LANES, NL), jnp.float32)],
        compiler_params=_params(("parallel",)),
        name="nsa_core",
    )(q, gates, kc, vct, ks, vst, kw, vwt, bcmp, bnear, bwin)


def _out_proj_kernel(a_ref, b_ref, x_ref, wa_ref, wb_ref, g_ref, beta_ref, o_ref):
    mix = jnp.dot(a_ref[...], wa_ref[...], preferred_element_type=jnp.float32)
    mix = mix + jnp.dot(b_ref[...], wb_ref[...], preferred_element_type=jnp.float32)
    o_ref[...] = _layer_norm(ALPHA * x_ref[...] + mix, g_ref[...], beta_ref[...])


def _out_proj(a, b, x, wa, wb, g, beta, tm=512):
    t, d = x.shape
    return pl.pallas_call(
        _out_proj_kernel,
        grid=(t // tm,),
        in_specs=[pl.BlockSpec((tm, a.shape[1]), lambda i: (i, 0)),
                  pl.BlockSpec((tm, b.shape[1]), lambda i: (i, 0)),
                  pl.BlockSpec((tm, d), lambda i: (i, 0)),
                  _const_spec(wa.shape), _const_spec(wb.shape),
                  _const_spec(g.shape), _const_spec(beta.shape)],
        out_specs=pl.BlockSpec((tm, d), lambda i: (i, 0)),
        out_shape=jax.ShapeDtypeStruct((t, d), jnp.float32),
        compiler_params=_params(("parallel",)),
        name="ab_out_proj_norm",
    )(a, b, x, wa, wb, g, beta)


def _ffn_kernel(x_ref, wg_ref, wu_ref, cwg_ref, cwu_ref, cbg_ref, cbu_ref, wd_ref, g_ref, beta_ref,
                o_ref, xb_ref, acc_ref, hbuf_ref, carry_ref):
    i, j = pl.program_id(0), pl.program_id(1)
    tm = x_ref.shape[0]
    width = cwg_ref.shape[0]

    @pl.when(j == 0)
    def _():
        xb_ref[...] = x_ref[...].astype(jnp.bfloat16)
        acc_ref[...] = jnp.zeros_like(acc_ref)

    @pl.when(i == 0)
    def _():
        carry_ref[:, j] = jnp.zeros((2,) + carry_ref.shape[2:], jnp.float32)

    def conv(part, w_ref, cw_ref, cb_ref):
        h = jnp.dot(xb_ref[...], w_ref[...], preferred_element_type=jnp.float32)
        hbuf_ref[0:SUBLANES, :] = carry_ref[part, j]
        hbuf_ref[SUBLANES:, :] = h
        carry_ref[part, j] = h[tm - SUBLANES:]
        y = cb_ref[...] + cw_ref[width - 1:width, :] * h
        for k in range(width - 1):
            off = SUBLANES - (width - 1) + k
            y = y + cw_ref[k:k + 1, :] * hbuf_ref[off:off + tm, :]
        return y

    yg = conv(0, wg_ref, cwg_ref, cbg_ref)
    yu = conv(1, wu_ref, cwu_ref, cbu_ref)
    act = (_gelu(yg) * yu).astype(jnp.bfloat16)
    acc_ref[...] += jnp.dot(act, wd_ref[...], preferred_element_type=jnp.float32)

    @pl.when(j == pl.num_programs(1) - 1)
    def _():
        o_ref[...] = _layer_norm(ALPHA * x_ref[...] + acc_ref[...], g_ref[...], beta_ref[...])


def _conv_ffn(x, w_up, conv_w, conv_b, w_down, g, beta, tm=512, fc=256):
    t, d = x.shape
    f = w_down.shape[0]
    nf = f // fc
    assert f % fc == 0 and t % tm == 0
    width = conv_w.shape[0]
    return pl.pallas_call(
        _ffn_kernel,
        grid=(t // tm, nf),
        in_specs=[pl.BlockSpec((tm, d), lambda i, j: (i, 0)),
                  pl.BlockSpec((d, fc), lambda i, j: (0, j)),
                  pl.BlockSpec((d, fc), lambda i, j: (0, nf + j)),
                  pl.BlockSpec((width, fc), lambda i, j: (0, j)),
                  pl.BlockSpec((width, fc), lambda i, j: (0, nf + j)),
                  pl.BlockSpec((1, fc), lambda i, j: (0, j)),
                  pl.BlockSpec((1, fc), lambda i, j: (0, nf + j)),
                  pl.BlockSpec((fc, d), lambda i, j: (j, 0)),
                  _const_spec(g.shape), _const_spec(beta.shape)],
        out_specs=pl.BlockSpec((tm, d), lambda i, j: (i, 0)),
        out_shape=jax.ShapeDtypeStruct((t, d), jnp.float32),
        scratch_shapes=[pltpu.VMEM((tm, d), jnp.bfloat16),
                        pltpu.VMEM((tm, d), jnp.float32),
                        pltpu.VMEM((tm + SUBLANES, fc), jnp.float32),
                        pltpu.VMEM((2, nf, SUBLANES, fc), jnp.float32)],
        compiler_params=_params(("arbitrary", "arbitrary")),
        name="conv_ffn_norm",
    )(x, w_up, w_up, conv_w, conv_w, conv_b, conv_b, w_down, g, beta)


def _conformer_kernel(x_ref, win_ref, bin_ref, dw_ref, dwb_ref, ng_ref, nb_ref, wout_ref, g_ref, beta_ref,
                      o_ref, gbuf_ref):
    i = pl.program_id(0)
    tm, d = x_ref.shape
    width = dw_ref.shape[0]
    halo = gbuf_ref.shape[0] - tm

    @pl.when(i == 0)
    def _():
        gbuf_ref[0:halo, :] = jnp.zeros((halo, d), jnp.float32)

    x = x_ref[...]
    h = jnp.dot(x.astype(jnp.bfloat16), win_ref[...], preferred_element_type=jnp.float32) + bin_ref[...]
    gbuf_ref[halo:, :] = h[:, :d] * _sigmoid(h[:, d:])

    base = halo - (width - 1)
    y = dwb_ref[...]
    for b in range(min(SUBLANES, width)):
        taps = range(b, width, SUBLANES)
        z = gbuf_ref[base + b:base + taps[-1] + tm, :]
        for k in taps:
            y = y + dw_ref[k:k + 1, :] * z[k - b:k - b + tm]
    gbuf_ref[0:halo, :] = gbuf_ref[tm:tm + halo, :]

    y = _layer_norm(y, ng_ref[...], nb_ref[...])
    y = (y * _sigmoid(y)).astype(jnp.bfloat16)
    mix = jnp.dot(y, wout_ref[...], preferred_element_type=jnp.float32)
    o_ref[...] = _layer_norm(ALPHA * x + mix, g_ref[...], beta_ref[...])


def _conformer(x, w_in, b_in, dw_w, dw_b, ng, nb, w_out, g, beta, tm=512):
    t, d = x.shape
    width = dw_w.shape[0]
    halo = -(-(width - 1) // SUBLANES) * SUBLANES
    consts = (w_in, b_in, dw_w, dw_b, ng, nb, w_out, g, beta)
    return pl.pallas_call(
        _conformer_kernel,
        grid=(t // tm,),
        in_specs=[pl.BlockSpec((tm, d), lambda i: (i, 0))] + [_const_spec(c.shape) for c in consts],
        out_specs=pl.BlockSpec((tm, d), lambda i: (i, 0)),
        out_shape=jax.ShapeDtypeStruct((t, d), jnp.float32),
        scratch_shapes=[pltpu.VMEM((tm + halo, d), jnp.float32)],
        compiler_params=_params(("arbitrary",)),
        name="conformer_conv_norm",
    )(x, *consts)


def _bias_by_distance(rel_table, n):
    dist = jnp.arange(n)
    max_exact = REL_BUCKETS // 2
    nf = jnp.maximum(dist, max_exact).astype(jnp.float32)
    large = max_exact + (jnp.log(nf / max_exact) / math.log(REL_MAX_DIST / max_exact)
                         * (REL_BUCKETS - max_exact)).astype(jnp.int32)
    large = jnp.minimum(large, REL_BUCKETS - 1)
    bucket = jnp.where(dist < max_exact, dist, large)
    return rel_table[bucket] - rel_table[REL_BUCKETS - 1][None, :]


def _bias_tile(fd, dist):
    n = fd.shape[0]
    b = fd[jnp.clip(dist, 0, n - 1)]
    b = jnp.where((dist >= 0)[..., None], b, NEG)
    return b.transpose(0, 2, 1).reshape(dist.shape[0], B_HEADS * Q_BLOCK)


def _attention_tables(rel_table):
    n = NEAR_ROWS + Q_BLOCK
    fd = _bias_by_distance(rel_table, n)
    i = jnp.arange(Q_BLOCK)[None, :]
    bnear = _bias_tile(fd, i + NEAR_BACK - jnp.arange(NEAR_ROWS)[:, None])
    dw = i + WINDOW - jnp.arange(WIN_ROWS)[:, None]
    bwin = _bias_tile(fd, jnp.where(dw < WINDOW, dw, -1))
    per_q = Q_BLOCK // CMP_STRIDE
    tiles = []
    for qb in range(CMP_SPECIAL):
        c0 = max(qb * per_q - (CMP_WIN - per_q), 0)
        cend = (c0 + jnp.arange(CMP_WIN)[:, None]) * CMP_STRIDE + CMP_BLOCK - 1
        tiles.append(_bias_tile(fd, qb * Q_BLOCK + i - cend))
    return jnp.stack(tiles), bnear, bwin


def _ab_weights(w_in, w_out):
    d = w_in.shape[0]
    o = 2 * A_WIDTH
    w_uv = w_in[:, :o]
    w_q = w_in[:, o:o + B_WIDTH].reshape(d, B_HEADS, B_HEAD_DIM)
    o += B_WIDTH
    q_pad = jnp.zeros((d, B_HEADS, B_KV_GROUPS, B_HEAD_DIM), w_in.dtype)
    for h in range(B_HEADS):
        q_pad = q_pad.at[:, h, h // B_HPG].set(w_q[:, h])
    w_kv = w_in[:, o:o + 6 * KV_WIDTH]
    o += 6 * KV_WIDTH
    w_g = w_in[:, o:].reshape(d, B_HEADS, 3).transpose(0, 2, 1).reshape(d, 3 * B_HEADS)
    w_g = jnp.pad(w_g, ((0, 0), (0, LANES - 3 * B_HEADS)))
    w = jnp.concatenate([w_uv, q_pad.reshape(d, B_HEADS * LANES), w_kv, w_g], axis=1)
    return w.astype(jnp.bfloat16), w_out[:A_WIDTH].astype(jnp.bfloat16), w_out[A_WIDTH:].astype(jnp.bfloat16)


def _compress_weights(pe_k, w1_k, w2_k, pe_v, w1_v, w2_v):
    hid = w1_k.shape[1]
    nseg = 2 * B_KV_GROUPS
    halves = []
    pes = []
    for half in range(CMP_BLOCK // CMP_STRIDE):
        rows = slice(half * CMP_STRIDE, (half + 1) * CMP_STRIDE)
        w = jnp.zeros((CMP_STRIDE, nseg, B_HEAD_DIM, nseg, hid), jnp.float32)
        pe = jnp.zeros((CMP_STRIDE, nseg, B_HEAD_DIM), jnp.float32)
        for seg in range(nseg):
            w1, pe_src = (w1_k, pe_k) if seg < B_KV_GROUPS else (w1_v, pe_v)
            w = w.at[:, seg, :, seg, :].set(w1.reshape(CMP_BLOCK, B_HEAD_DIM, hid)[rows])
            pe = pe.at[:, seg, :].set(pe_src[rows])
        halves.append(w.reshape(CMP_STRIDE * nseg * B_HEAD_DIM, nseg * hid).astype(jnp.bfloat16))
        pes.append(pe.reshape(1, CMP_STRIDE * nseg * B_HEAD_DIM))
    w2 = jnp.zeros((nseg, hid, nseg, B_HEAD_DIM), jnp.float32)
    for seg in range(nseg):
        w2 = w2.at[seg, :, seg, :].set(w2_k if seg < B_KV_GROUPS else w2_v)
    return pes[0], pes[1], halves[0], halves[1], w2.reshape(nseg * hid, nseg * B_HEAD_DIM).astype(jnp.bfloat16)


def _row(v):
    return v.reshape(1, -1)


def _mixer_ab_layer(x, rel_table, w_in, sgu_ln_g, sgu_ln_b, sgu_w, sgu_b,
                    pe_k, w1_k, w2_k, pe_v, w1_v, w2_v, w_out, ln_g, ln_b):
    t = x.shape[0]
    w_all, w_out_a, w_out_b = _ab_weights(w_in, w_out)
    uv, q, cmp_in, kv, gates = _proj(x, w_all)

    causal = jnp.tril(jnp.ones((A_CHUNK, A_CHUNK), dtype=bool))
    ws = jnp.where(causal[None], sgu_w, 0.0).astype(jnp.bfloat16)
    sgu_bias = jnp.repeat(sgu_b.T, A_GROUP_DIM, axis=1)
    a_out = _sgu(uv, _row(sgu_ln_g), _row(sgu_ln_b), ws, sgu_bias)

    pea, peb, wa, wb, w2 = _compress_weights(pe_k, w1_k, w2_k, pe_v, w1_v, w2_v)
    cmp_out = _compress(cmp_in.reshape(t // CMP_STRIDE, CMP_STRIDE * 2 * KV_WIDTH), pea, peb, wa, wb, w2)
    kc = cmp_out[:, :KV_WIDTH].astype(jnp.bfloat16)
    vct = cmp_out[:, KV_WIDTH:].T.astype(jnp.bfloat16)

    ks, vs, kw, vw = (kv[:, n * KV_WIDTH:(n + 1) * KV_WIDTH] for n in range(4))
    kw_pad = jnp.pad(kw, ((WINDOW, 0), (0, 0)))
    vwt_pad = jnp.pad(vw, ((WINDOW, 0), (0, 0))).T
    bcmp, bnear, bwin = _attention_tables(rel_table)
    b_out = _nsa(q, gates, kc, vct, ks, vs.T, kw_pad, vwt_pad, bcmp, bnear, bwin)

    return _out_proj(a_out, b_out, x, w_out_a, w_out_b, _row(ln_g), _row(ln_b))


def kernel(x, rel_table, ab_w_in, ab_sgu_ln_g, ab_sgu_ln_b, ab_sgu_w, ab_sgu_b, ab_cmp_pe_k, ab_cmp_w1_k, ab_cmp_w2_k, ab_cmp_pe_v, ab_cmp_w1_v, ab_cmp_w2_v, ab_w_out, c_w_in, c_b_in, c_dw_w, c_dw_b, c_norm_g, c_norm_b, c_w_out, ffn_w_up, ffn_conv_w, ffn_conv_b, ffn_w_down, ln_mix_g, ln_mix_b, ln_ffn_g, ln_ffn_b):
    bsz = x.shape[0]
    outs = []
    for bi in range(bsz):
        h = x[bi]
        for layer in range(DEPTH):
            i = layer // 2
            if layer % 2 == 0:
                h = _mixer_ab_layer(h, rel_table, ab_w_in[i], ab_sgu_ln_g[i], ab_sgu_ln_b[i], ab_sgu_w[i],
                                    ab_sgu_b[i], ab_cmp_pe_k[i], ab_cmp_w1_k[i], ab_cmp_w2_k[i],
                                    ab_cmp_pe_v[i], ab_cmp_w1_v[i], ab_cmp_w2_v[i], ab_w_out[i],
                                    ln_mix_g[layer], ln_mix_b[layer])
            else:
                h = _conformer(h, c_w_in[i].astype(jnp.bfloat16), _row(c_b_in[i]), c_dw_w[i], _row(c_dw_b[i]),
                               _row(c_norm_g[i]), _row(c_norm_b[i]), c_w_out[i].astype(jnp.bfloat16),
                               _row(ln_mix_g[layer]), _row(ln_mix_b[layer]))
            h = _conv_ffn(h, ffn_w_up[layer].astype(jnp.bfloat16), ffn_conv_w[layer], _row(ffn_conv_b[layer]),
                          ffn_w_down[layer].astype(jnp.bfloat16), _row(ln_ffn_g[layer]), _row(ln_ffn_b[layer]))
        outs.append(h)
    return jnp.stack(outs)
```

```python
import functools
import math

import numpy as np
import jax
import jax.numpy as jnp
from jax import lax
from jax.experimental import pallas as pl
from jax.experimental.pallas import tpu as pltpu

DEPTH = 2
A_GROUPS = 8
A_GROUP_DIM = 64
A_CHUNK = 128
B_HEADS = 8
B_HEAD_DIM = 64
B_KV_GROUPS = 2
B_HPG = B_HEADS // B_KV_GROUPS
CMP_BLOCK = 32
CMP_STRIDE = 16
SLC_BLOCK = 64
SLC_TOP_N = 16
WINDOW = 512
Q_BLOCK = 128
FORCED_SCORE = 1e4
REL_BUCKETS = 32
REL_MAX_DIST = 2048
LN_EPS = 1e-5
ALPHA = (2 * DEPTH) ** 0.25

A_WIDTH = A_GROUPS * A_GROUP_DIM
B_WIDTH = B_HEADS * B_HEAD_DIM
KV_WIDTH = B_KV_GROUPS * B_HEAD_DIM

LANES = 128
SUBLANES = 8
VMEM_LIMIT = 56 * 1024 * 1024

NEG = -1e30
M_INIT = -5e29
QL = B_HPG * Q_BLOCK
NL = B_KV_GROUPS * QL
KEY_TILE = 512
SLC_PER_TILE = KEY_TILE // SLC_BLOCK
NEAR_BACK = 2048
NEAR_ROWS = NEAR_BACK + KEY_TILE
WIN_ROWS = WINDOW + Q_BLOCK
CMP_WIN = 128
CMP_SPECIAL = 16
SUB_ROWS = 32


def _bucket_starts():
    n = np.arange(2 * REL_MAX_DIST)
    max_exact = REL_BUCKETS // 2
    nf = np.maximum(n, max_exact).astype(np.float64)
    large = max_exact + np.floor(np.log(nf / max_exact) / math.log(REL_MAX_DIST / max_exact)
                                 * (REL_BUCKETS - max_exact)).astype(np.int64)
    bucket = np.where(n < max_exact, n, np.minimum(large, REL_BUCKETS - 1))
    return tuple(int(np.argmax(bucket >= b)) for b in range(REL_BUCKETS))


BUCKET_START = _bucket_starts()
FAR_DIST = BUCKET_START[-1]


def _gelu(x):
    c = math.sqrt(2.0 / math.pi)
    return x * (0.5 * (1.0 + jnp.tanh(c * (x + 0.044715 * (x * x * x)))))


def _layer_norm(x, g, b):
    mu = jnp.mean(x, axis=-1, keepdims=True)
    xc = x - mu
    var = jnp.mean(xc * xc, axis=-1, keepdims=True)
    return xc * lax.rsqrt(var + LN_EPS) * g + b


def _sigmoid(x):
    return 1.0 / (1.0 + jnp.exp(-x))


def _const_spec(shape):
    n = len(shape)
    return pl.BlockSpec(shape, lambda *_: (0,) * n, pipeline_mode=pl.Buffered(1))


def _params(sem):
    return pltpu.CompilerParams(dimension_semantics=sem, vmem_limit_bytes=VMEM_LIMIT)


def _proj_kernel(x_ref, w_ref, uv_ref, q_ref, cmp_ref, kv_ref, gate_ref):
    xb = x_ref[...].astype(jnp.bfloat16)
    o = 0
    for ref, scale in ((uv_ref, None), (q_ref, B_HEAD_DIM ** -0.5), (cmp_ref, None), (kv_ref, None),
                       (gate_ref, None)):
        n = ref.shape[1]
        h = jnp.dot(xb, w_ref[:, o:o + n], preferred_element_type=jnp.float32)
        if scale is not None:
            h = h * scale
        ref[...] = h.astype(ref.dtype)
        o += n


def _proj(x, w, tm=512):
    t, d = x.shape
    widths = (2 * A_WIDTH, B_HEADS * LANES, 2 * KV_WIDTH, 4 * KV_WIDTH, LANES)
    dtypes = (jnp.float32, jnp.bfloat16, jnp.float32, jnp.bfloat16, jnp.float32)
    assert w.shape == (d, sum(widths)) and t % tm == 0
    return pl.pallas_call(
        _proj_kernel,
        grid=(t // tm,),
        in_specs=[pl.BlockSpec((tm, d), lambda i: (i, 0)), _const_spec(w.shape)],
        out_specs=[pl.BlockSpec((tm, n), lambda i: (i, 0)) for n in widths],
        out_shape=[jax.ShapeDtypeStruct((t, n), dt) for n, dt in zip(widths, dtypes)],
        compiler_params=_params(("parallel",)),
        name="ab_in_proj",
    )(x, w)


def _sgu_kernel(u_ref, v_ref, g_ref, b_ref, ws_ref, bias_ref, o_ref):
    tm = u_ref.shape[0]
    u = _gelu(u_ref[...])
    v = _layer_norm(_gelu(v_ref[...]), g_ref[...], b_ref[...]).astype(jnp.bfloat16)
    lane_group = lax.broadcasted_iota(jnp.int32, (A_CHUNK, A_WIDTH), 1) // A_GROUP_DIM
    for c in range(tm // A_CHUNK):
        rows = slice(c * A_CHUNK, (c + 1) * A_CHUNK)
        vc = v[rows]
        s = bias_ref[...]
        for g in range(A_GROUPS):
            sg = jnp.dot(ws_ref[g], vc, preferred_element_type=jnp.float32)
            s = s + jnp.where(lane_group == g, sg, 0.0)
        o_ref[rows, :] = (u[rows] * s).astype(o_ref.dtype)


def _sgu(uv, ln_g, ln_b, ws, bias, tm=512):
    t = uv.shape[0]
    return pl.pallas_call(
        _sgu_kernel,
        grid=(t // tm,),
        in_specs=[pl.BlockSpec((tm, A_WIDTH), lambda i: (i, 0)),
                  pl.BlockSpec((tm, A_WIDTH), lambda i: (i, 1)),
                  _const_spec(ln_g.shape), _const_spec(ln_b.shape),
                  _const_spec(ws.shape), _const_spec(bias.shape)],
        out_specs=pl.BlockSpec((tm, A_WIDTH), lambda i: (i, 0)),
        out_shape=jax.ShapeDtypeStruct((t, A_WIDTH), jnp.bfloat16),
        compiler_params=_params(("parallel",)),
        name="sgu_mixer",
    )(uv, uv, ln_g, ln_b, ws, bias)


def _compress_kernel(r_ref, nxt_ref, pea_ref, peb_ref, wa_ref, wb_ref, w2_ref, o_ref):
    tm = r_ref.shape[0]
    r = r_ref[...]
    xa = (r + pea_ref[...]).astype(jnp.bfloat16)
    xb = (jnp.concatenate([r, nxt_ref[...]], axis=0) + peb_ref[...]).astype(jnp.bfloat16)
    ha = jnp.dot(xa, wa_ref[...], preferred_element_type=jnp.float32)
    hb = jnp.dot(xb, wb_ref[...], preferred_element_type=jnp.float32)
    hid = _gelu(ha + hb[1:tm + 1]).astype(jnp.bfloat16)
    o_ref[...] = jnp.dot(hid, w2_ref[...], preferred_element_type=jnp.float32)


def _compress(r, pea, peb, wa, wb, w2, tm=256):
    n, width = r.shape
    nblk = n // SUBLANES
    return pl.pallas_call(
        _compress_kernel,
        grid=(n // tm,),
        in_specs=[pl.BlockSpec((tm, width), lambda i: (i, 0)),
                  pl.BlockSpec((SUBLANES, width),
                               lambda i: (jnp.minimum((i + 1) * (tm // SUBLANES), nblk - 1), 0)),
                  _const_spec(pea.shape), _const_spec(peb.shape),
                  _const_spec(wa.shape), _const_spec(wb.shape), _const_spec(w2.shape)],
        out_specs=pl.BlockSpec((tm, w2.shape[1]), lambda i: (i, 0)),
        out_shape=jax.ShapeDtypeStruct((n, w2.shape[1]), jnp.float32),
        compiler_params=_params(("parallel",)),
        name="kv_compress",
    )(r, r, pea, peb, wa, wb, w2)


def _softmax_cols(l):
    m = jnp.maximum(jnp.max(l, axis=0, keepdims=True), M_INIT)
    e = jnp.exp(l - m)
    return e, jnp.sum(e, axis=0, keepdims=True)


def _nsa_kernel(q_ref, gate_ref, kc_ref, vct_ref, ks_ref, vst_ref, kw_ref, vwt_ref,
                bcmp_ref, bnear_ref, bwin_ref, o_ref,
                sc_ref, pcs_ref, sel_ref, acc_ref):
    qb = pl.program_id(0)
    s0 = qb * Q_BLOCK
    n_cmp = kc_ref.shape[0]
    n_slc = sel_ref.shape[1]

    q_all = jnp.concatenate([q_ref[:, h * LANES:(h + 1) * LANES] for h in range(B_HEADS)], axis=0)

    def scores(k):
        return lax.dot_general(k, q_all, (((1,), (1,)), ((), ())), preferred_element_type=jnp.float32)

    c0 = jnp.maximum(qb * (Q_BLOCK // CMP_STRIDE) - (CMP_WIN - Q_BLOCK // CMP_STRIDE), 0)
    c0 = pl.multiple_of(c0, SUBLANES)
    row_c = lax.broadcasted_iota(jnp.int32, (n_cmp, 1), 0)
    sc_ref[...] = jnp.where(row_c < c0 + CMP_WIN, scores(kc_ref[...]), NEG)
    sc_ref[pl.ds(c0, CMP_WIN), :] += bcmp_ref[0]
    e_c, sum_c = _softmax_cols(sc_ref[...])
    inv_c = 1.0 / jnp.maximum(sum_c, 1e-30)
    oc = jnp.dot(vct_ref[...], e_c.astype(jnp.bfloat16), preferred_element_type=jnp.float32)

    pc = e_c * inv_c
    t_row = s0 + lax.broadcasted_iota(jnp.int32, (1, Q_BLOCK), 1)
    blk = lax.broadcasted_iota(jnp.int32, (n_slc, Q_BLOCK), 0)
    blk_f = blk.astype(jnp.float32)
    own = blk == t_row // SLC_BLOCK
    valid = blk * SLC_BLOCK <= t_row
    ratio = SLC_BLOCK // CMP_STRIDE
    pcs_ref[0:SUBLANES, :] = jnp.zeros((SUBLANES, Q_BLOCK), jnp.float32)
    for g in range(B_KV_GROUPS):
        pg = pc[:, g * QL:(g + 1) * QL]
        pcs_ref[SUBLANES:, :] = sum(pg[:, p * Q_BLOCK:(p + 1) * Q_BLOCK] for p in range(B_HPG))
        score = pcs_ref[pl.ds(SUBLANES - 1, n_slc, stride=ratio), :]
        for r in range(ratio):
            score = score + pcs_ref[pl.ds(SUBLANES + r, n_slc, stride=ratio), :]
        score = jnp.where(valid, score, -1.0)
        score = jnp.where(own, FORCED_SCORE, jnp.where(blk == 0, FORCED_SCORE, score))
        sel = jnp.full((n_slc, Q_BLOCK), NEG, jnp.float32)
        for _ in range(min(SLC_TOP_N, n_slc)):
            best = jnp.max(score, axis=0, keepdims=True)
            first = jnp.min(jnp.where(score == best, blk_f, float(n_slc)), axis=0, keepdims=True)
            hit = blk_f == first
            sel = jnp.where(hit, 0.0, sel)
            score = jnp.where(hit, -2.0, score)
        sel_ref[g] = sel

    def sel_mask(m):
        parts = []
        for g in range(B_KV_GROUPS):
            rows = sel_ref[g, pl.ds(pl.multiple_of(m * SLC_PER_TILE, SUBLANES), SLC_PER_TILE), :]
            mg = jnp.concatenate([jnp.broadcast_to(rows[b:b + 1, :], (SLC_BLOCK, Q_BLOCK))
                                  for b in range(SLC_PER_TILE)], axis=0)
            parts.extend([mg] * B_HPG)
        return jnp.concatenate(parts, axis=1)

    def tile_step(m, carry, near):
        m_run, s_run = carry
        k0 = pl.multiple_of(m * KEY_TILE, KEY_TILE)
        l = scores(ks_ref[pl.ds(k0, KEY_TILE), :]) + sel_mask(m)
        if near:
            r0 = pl.multiple_of(k0 - s0 + NEAR_BACK, Q_BLOCK)
            l = l + bnear_ref[pl.ds(r0, KEY_TILE), :]
        m_new = jnp.maximum(m_run, jnp.max(l, axis=0, keepdims=True))
        alpha = jnp.exp(m_run - m_new)
        e = jnp.exp(l - m_new)
        s_new = alpha * s_run + jnp.sum(e, axis=0, keepdims=True)
        pv = jnp.dot(vst_ref[:, pl.ds(k0, KEY_TILE)], e.astype(jnp.bfloat16),
                     preferred_element_type=jnp.float32)
        acc_ref[...] = alpha * acc_ref[...] + pv
        return m_new, s_new

    m_far = jnp.maximum((s0 - (FAR_DIST - 1)) // KEY_TILE, 0)
    m_end = s0 // KEY_TILE + 1
    acc_ref[...] = jnp.zeros_like(acc_ref)
    carry = (jnp.full((1, NL), M_INIT, jnp.float32), jnp.zeros((1, NL), jnp.float32))
    carry = lax.fori_loop(0, m_far, functools.partial(tile_step, near=False), carry)
    _, sum_s = lax.fori_loop(m_far, m_end, functools.partial(tile_step, near=True), carry)
    osl = acc_ref[...]
    inv_s = 1.0 / jnp.maximum(sum_s, 1e-30)

    k0w = pl.multiple_of(s0, Q_BLOCK)
    row_w = lax.broadcasted_iota(jnp.int32, (WIN_ROWS, 1), 0)
    lw = scores(kw_ref[pl.ds(k0w, WIN_ROWS), :]) + bwin_ref[...]
    lw = jnp.where(row_w >= WINDOW - s0, lw, NEG)
    e_w, sum_w = _softmax_cols(lw)
    inv_w = 1.0 / jnp.maximum(sum_w, 1e-30)
    ow = jnp.dot(vwt_ref[:, pl.ds(k0w, WIN_ROWS)], e_w.astype(jnp.bfloat16),
                 preferred_element_type=jnp.float32)

    gt = _sigmoid(gate_ref[...]).T

    def gate_row(k):
        return jnp.concatenate([gt[k * B_HEADS + h:k * B_HEADS + h + 1, :] for h in range(B_HEADS)], axis=1)

    out_t = (oc * (gate_row(0) * inv_c) + osl * (gate_row(1) * inv_s) + ow * (gate_row(2) * inv_w))
    heads = []
    for h in range(B_HEADS):
        g = h // B_HPG
        blk_t = out_t[g * B_HEAD_DIM:(g + 1) * B_HEAD_DIM, h * Q_BLOCK:(h + 1) * Q_BLOCK]
        heads.append(blk_t.T)
    o_ref[...] = jnp.concatenate(heads, axis=1).astype(o_ref.dtype)


def _nsa(q, gates, kc, vct, ks, vst, kw, vwt, bcmp, bnear, bwin):
    t = q.shape[0]
    n_cmp = kc.shape[0]
    n_slc = t // SLC_BLOCK
    return pl.pallas_call(
        _nsa_kernel,
        grid=(t // Q_BLOCK,),
        in_specs=[pl.BlockSpec((Q_BLOCK, B_HEADS * LANES), lambda i: (i, 0)),
                  pl.BlockSpec((Q_BLOCK, LANES), lambda i: (i, 0)),
                  _const_spec(kc.shape), _const_spec(vct.shape),
                  _const_spec(ks.shape), _const_spec(vst.shape),
                  _const_spec(kw.shape), _const_spec(vwt.shape),
                  pl.BlockSpec((1, CMP_WIN, NL), lambda i: (jnp.minimum(i, CMP_SPECIAL - 1), 0, 0)),
                  _const_spec(bnear.shape), _const_spec(bwin.shape)],
        out_specs=pl.BlockSpec((Q_BLOCK, B_WIDTH), lambda i: (i, 0)),
        out_shape=jax.ShapeDtypeStruct((t, B_WIDTH), jnp.bfloat16),
        scratch_shapes=[pltpu.VMEM((n_cmp, NL), jnp.float32),
                        pltpu.VMEM((n_cmp + SUBLANES, Q_BLOCK), jnp.float32),
                        pltpu.VMEM((B_KV_GROUPS, n_slc, Q_BLOCK), jnp.float32),
                        pltpu.VMEM((LANES, NL), jnp.float32)],
        compiler_params=_params(("parallel",)),
        name="nsa_core",
    )(q, gates, kc, vct, ks, vst, kw, vwt, bcmp, bnear, bwin)


def _out_proj_kernel(a_ref, b_ref, x_ref, wa_ref, wb_ref, g_ref, beta_ref, o_ref):
    mix = jnp.dot(a_ref[...], wa_ref[...], preferred_element_type=jnp.float32)
    mix = mix + jnp.dot(b_ref[...], wb_ref[...], preferred_element_type=jnp.float32)
    o_ref[...] = _layer_norm(ALPHA * x_ref[...] + mix, g_ref[...], beta_ref[...])


def _out_proj(a, b, x, wa, wb, g, beta, tm=512):
    t, d = x.shape
    return pl.pallas_call(
        _out_proj_kernel,
        grid=(t // tm,),
        in_specs=[pl.BlockSpec((tm, a.shape[1]), lambda i: (i, 0)),
                  pl.BlockSpec((tm, b.shape[1]), lambda i: (i, 0)),
                  pl.BlockSpec((tm, d), lambda i: (i, 0)),
                  _const_spec(wa.shape), _const_spec(wb.shape),
                  _const_spec(g.shape), _const_spec(beta.shape)],
        out_specs=pl.BlockSpec((tm, d), lambda i: (i, 0)),
        out_shape=jax.ShapeDtypeStruct((t, d), jnp.float32),
        compiler_params=_params(("parallel",)),
        name="ab_out_proj_norm",
    )(a, b, x, wa, wb, g, beta)


def _ffn_kernel(x_ref, wg_ref, wu_ref, cwg_ref, cwu_ref, cbg_ref, cbu_ref, wd_ref, g_ref, beta_ref,
                o_ref, xb_ref, acc_ref, hbuf_ref, carry_ref):
    i, j = pl.program_id(0), pl.program_id(1)
    tm = x_ref.shape[0]
    width = cwg_ref.shape[0]

    @pl.when(j == 0)
    def _():
        xb_ref[...] = x_ref[...].astype(jnp.bfloat16)
        acc_ref[...] = jnp.zeros_like(acc_ref)

    @pl.when(i == 0)
    def _():
        carry_ref[:, j] = jnp.zeros((2,) + carry_ref.shape[2:], jnp.float32)

    def conv(part, w_ref, cw_ref, cb_ref):
        h = jnp.dot(xb_ref[...], w_ref[...], preferred_element_type=jnp.float32)
        hbuf_ref[0:SUBLANES, :] = carry_ref[part, j]
        hbuf_ref[SUBLANES:, :] = h
        carry_ref[part, j] = h[tm - SUBLANES:]
        y = cb_ref[...] + cw_ref[width - 1:width, :] * h
        for k in range(width - 1):
            off = SUBLANES - (width - 1) + k
            y = y + cw_ref[k:k + 1, :] * hbuf_ref[off:off + tm, :]
        return y

    yg = conv(0, wg_ref, cwg_ref, cbg_ref)
    yu = conv(1, wu_ref, cwu_ref, cbu_ref)
    act = (_gelu(yg) * yu).astype(jnp.bfloat16)
    acc_ref[...] += jnp.dot(act, wd_ref[...], preferred_element_type=jnp.float32)

    @pl.when(j == pl.num_programs(1) - 1)
    def _():
        o_ref[...] = _layer_norm(ALPHA * x_ref[...] + acc_ref[...], g_ref[...], beta_ref[...])


def _conv_ffn(x, w_up, conv_w, conv_b, w_down, g, beta, tm=512, fc=256):
    t, d = x.shape
    f = w_down.shape[0]
    nf = f // fc
    assert f % fc == 0 and t % tm == 0
    width = conv_w.shape[0]
    return pl.pallas_call(
        _ffn_kernel,
        grid=(t // tm, nf),
        in_specs=[pl.BlockSpec((tm, d), lambda i, j: (i, 0)),
                  pl.BlockSpec((d, fc), lambda i, j: (0, j)),
                  pl.BlockSpec((d, fc), lambda i, j: (0, nf + j)),
                  pl.BlockSpec((width, fc), lambda i, j: (0, j)),
                  pl.BlockSpec((width, fc), lambda i, j: (0, nf + j)),
                  pl.BlockSpec((1, fc), lambda i, j: (0, j)),
                  pl.BlockSpec((1, fc), lambda i, j: (0, nf + j)),
                  pl.BlockSpec((fc, d), lambda i, j: (j, 0)),
                  _const_spec(g.shape), _const_spec(beta.shape)],
        out_specs=pl.BlockSpec((tm, d), lambda i, j: (i, 0)),
        out_shape=jax.ShapeDtypeStruct((t, d), jnp.float32),
        scratch_shapes=[pltpu.VMEM((tm, d), jnp.bfloat16),
                        pltpu.VMEM((tm, d), jnp.float32),
                        pltpu.VMEM((tm + SUBLANES, fc), jnp.float32),
                        pltpu.VMEM((2, nf, SUBLANES, fc), jnp.float32)],
        compiler_params=_params(("arbitrary", "arbitrary")),
        name="conv_ffn_norm",
    )(x, w_up, w_up, conv_w, conv_w, conv_b, conv_b, w_down, g, beta)


def _conformer_kernel(x_ref, win_ref, bin_ref, dw_ref, dwb_ref, ng_ref, nb_ref, wout_ref, g_ref, beta_ref,
                      o_ref, gbuf_ref):
    i = pl.program_id(0)
    tm, d = x_ref.shape
    width = dw_ref.shape[0]
    halo = gbuf_ref.shape[0] - tm

    @pl.when(i == 0)
    def _():
        gbuf_ref[0:halo, :] = jnp.zeros((halo, d), jnp.float32)

    x = x_ref[...]
    h = jnp.dot(x.astype(jnp.bfloat16), win_ref[...], preferred_element_type=jnp.float32) + bin_ref[...]
    gbuf_ref[halo:, :] = h[:, :d] * _sigmoid(h[:, d:])

    base = halo - (width - 1)
    y = dwb_ref[...]
    for b in range(min(SUBLANES, width)):
        taps = range(b, width, SUBLANES)
        z = gbuf_ref[base + b:base + taps[-1] + tm, :]
        for k in taps:
            y = y + dw_ref[k:k + 1, :] * z[k - b:k - b + tm]
    gbuf_ref[0:halo, :] = gbuf_ref[tm:tm + halo, :]

    y = _layer_norm(y, ng_ref[...], nb_ref[...])
    y = (y * _sigmoid(y)).astype(jnp.bfloat16)
    mix = jnp.dot(y, wout_ref[...], preferred_element_type=jnp.float32)
    o_ref[...] = _layer_norm(ALPHA * x + mix, g_ref[...], beta_ref[...])


def _conformer(x, w_in, b_in, dw_w, dw_b, ng, nb, w_out, g, beta, tm=512):
    t, d = x.shape
    width = dw_w.shape[0]
    halo = -(-(width - 1) // SUBLANES) * SUBLANES
    consts = (w_in, b_in, dw_w, dw_b, ng, nb, w_out, g, beta)
    return pl.pallas_call(
        _conformer_kernel,
        grid=(t // tm,),
        in_specs=[pl.BlockSpec((tm, d), lambda i: (i, 0))] + [_const_spec(c.shape) for c in consts],
        out_specs=pl.BlockSpec((tm, d), lambda i: (i, 0)),
        out_shape=jax.ShapeDtypeStruct((t, d), jnp.float32),
        scratch_shapes=[pltpu.VMEM((tm + halo, d), jnp.float32)],
        compiler_params=_params(("arbitrary",)),
        name="conformer_conv_norm",
    )(x, *consts)


def _bias_kernel(tbl_ref, o_ref, *, kind):
    step = pl.program_id(0)
    rows = o_ref.shape[-2]
    out = o_ref if kind != "cmp" else o_ref.at[0]
    per_q = Q_BLOCK // CMP_STRIDE
    for sub in range(rows // SUB_ROWS):
        row = lax.broadcasted_iota(jnp.int32, (SUB_ROWS, Q_BLOCK), 0) + sub * SUB_ROWS
        lane = lax.broadcasted_iota(jnp.int32, (SUB_ROWS, Q_BLOCK), 1)
        if kind == "near":
            dist = lane + NEAR_BACK - (step * rows + row)
        elif kind == "win":
            dist = lane + WINDOW - (step * rows + row)
        else:
            c0 = jnp.maximum(step * per_q - (CMP_WIN - per_q), 0)
            dist = step * Q_BLOCK + lane - ((c0 + row) * CMP_STRIDE + CMP_BLOCK - 1)
        acc = [jnp.full((SUB_ROWS, Q_BLOCK), NEG, jnp.float32) for _ in range(B_HEADS)]
        for b in range(REL_BUCKETS):
            reached = dist >= BUCKET_START[b]
            for h in range(B_HEADS):
                acc[h] = jnp.where(reached, tbl_ref[b, h] - tbl_ref[REL_BUCKETS - 1, h], acc[h])
        for h in range(B_HEADS):
            if kind == "win":
                acc[h] = jnp.where(dist >= WINDOW, NEG, acc[h])
            out[sub * SUB_ROWS:(sub + 1) * SUB_ROWS, h * Q_BLOCK:(h + 1) * Q_BLOCK] = acc[h]


def _bias_table(rel_table, kind, shape, block):
    nd = len(shape)
    return pl.pallas_call(
        functools.partial(_bias_kernel, kind=kind),
        grid=(shape[0] // block[0],),
        in_specs=[pl.BlockSpec(memory_space=pltpu.SMEM)],
        out_specs=pl.BlockSpec(block, lambda i: (i,) + (0,) * (nd - 1)),
        out_shape=jax.ShapeDtypeStruct(shape, jnp.float32),
        compiler_params=_params(("parallel",)),
        name="rel_bias_" + kind,
    )(rel_table)


def _attention_tables(rel_table):
    bcmp = _bias_table(rel_table, "cmp", (CMP_SPECIAL, CMP_WIN, NL), (1, CMP_WIN, NL))
    bnear = _bias_table(rel_table, "near", (NEAR_ROWS, NL), (Q_BLOCK, NL))
    bwin = _bias_table(rel_table, "win", (WIN_ROWS, NL), (Q_BLOCK, NL))
    return bcmp, bnear, bwin


def _ab_weights(w_in, w_out):
    d = w_in.shape[0]
    o = 2 * A_WIDTH
    w_uv = w_in[:, :o]
    w_q = w_in[:, o:o + B_WIDTH].reshape(d, B_HEADS, B_HEAD_DIM)
    o += B_WIDTH
    q_pad = jnp.zeros((d, B_HEADS, B_KV_GROUPS, B_HEAD_DIM), w_in.dtype)
    for h in range(B_HEADS):
        q_pad = q_pad.at[:, h, h // B_HPG].set(w_q[:, h])
    w_kv = w_in[:, o:o + 6 * KV_WIDTH]
    o += 6 * KV_WIDTH
    w_g = w_in[:, o:].reshape(d, B_HEADS, 3).transpose(0, 2, 1).reshape(d, 3 * B_HEADS)
    w_g = jnp.pad(w_g, ((0, 0), (0, LANES - 3 * B_HEADS)))
    w = jnp.concatenate([w_uv, q_pad.reshape(d, B_HEADS * LANES), w_kv, w_g], axis=1)
    return w.astype(jnp.bfloat16), w_out[:A_WIDTH].astype(jnp.bfloat16), w_out[A_WIDTH:].astype(jnp.bfloat16)


def _compress_weights(pe_k, w1_k, w2_k, pe_v, w1_v, w2_v):
    hid = w1_k.shape[1]
    nseg = 2 * B_KV_GROUPS
    halves = []
    pes = []
    for half in range(CMP_BLOCK // CMP_STRIDE):
        rows = slice(half * CMP_STRIDE, (half + 1) * CMP_STRIDE)
        w = jnp.zeros((CMP_STRIDE, nseg, B_HEAD_DIM, nseg, hid), jnp.float32)
        pe = jnp.zeros((CMP_STRIDE, nseg, B_HEAD_DIM), jnp.float32)
        for seg in range(nseg):
            w1, pe_src = (w1_k, pe_k) if seg < B_KV_GROUPS else (w1_v, pe_v)
            w = w.at[:, seg, :, seg, :].set(w1.reshape(CMP_BLOCK, B_HEAD_DIM, hid)[rows])
            pe = pe.at[:, seg, :].set(pe_src[rows])
        halves.append(w.reshape(CMP_STRIDE * nseg * B_HEAD_DIM, nseg * hid).astype(jnp.bfloat16))
        pes.append(pe.reshape(1, CMP_STRIDE * nseg * B_HEAD_DIM))
    w2 = jnp.zeros((nseg, hid, nseg, B_HEAD_DIM), jnp.float32)
    for seg in range(nseg):
        w2 = w2.at[seg, :, seg, :].set(w2_k if seg < B_KV_GROUPS else w2_v)
    return pes[0], pes[1], halves[0], halves[1], w2.reshape(nseg * hid, nseg * B_HEAD_DIM).astype(jnp.bfloat16)


def _row(v):
    return v.reshape(1, -1)


def _mixer_ab_layer(x, rel_table, w_in, sgu_ln_g, sgu_ln_b, sgu_w, sgu_b,
                    pe_k, w1_k, w2_k, pe_v, w1_v, w2_v, w_out, ln_g, ln_b):
    t = x.shape[0]
    w_all, w_out_a, w_out_b = _ab_weights(w_in, w_out)
    uv, q, cmp_in, kv, gates = _proj(x, w_all)

    causal = jnp.tril(jnp.ones((A_CHUNK, A_CHUNK), dtype=bool))
    ws = jnp.where(causal[None], sgu_w, 0.0).astype(jnp.bfloat16)
    sgu_bias = jnp.repeat(sgu_b.T, A_GROUP_DIM, axis=1)
    a_out = _sgu(uv, _row(sgu_ln_g), _row(sgu_ln_b), ws, sgu_bias)

    pea, peb, wa, wb, w2 = _compress_weights(pe_k, w1_k, w2_k, pe_v, w1_v, w2_v)
    cmp_out = _compress(cmp_in.reshape(t // CMP_STRIDE, CMP_STRIDE * 2 * KV_WIDTH), pea, peb, wa, wb, w2)
    kc = cmp_out[:, :KV_WIDTH].astype(jnp.bfloat16)
    vct = cmp_out[:, KV_WIDTH:].T.astype(jnp.bfloat16)

    ks, vs, kw, vw = (kv[:, n * KV_WIDTH:(n + 1) * KV_WIDTH] for n in range(4))
    kw_pad = jnp.pad(kw, ((WINDOW, 0), (0, 0)))
    vwt_pad = jnp.pad(vw, ((WINDOW, 0), (0, 0))).T
    bcmp, bnear, bwin = _attention_tables(rel_table)
    b_out = _nsa(q, gates, kc, vct, ks, vs.T, kw_pad, vwt_pad, bcmp, bnear, bwin)

    return _out_proj(a_out, b_out, x, w_out_a, w_out_b, _row(ln_g), _row(ln_b))


def kernel(x, rel_table, ab_w_in, ab_sgu_ln_g, ab_sgu_ln_b, ab_sgu_w, ab_sgu_b, ab_cmp_pe_k, ab_cmp_w1_k, ab_cmp_w2_k, ab_cmp_pe_v, ab_cmp_w1_v, ab_cmp_w2_v, ab_w_out, c_w_in, c_b_in, c_dw_w, c_dw_b, c_norm_g, c_norm_b, c_w_out, ffn_w_up, ffn_conv_w, ffn_conv_b, ffn_w_down, ln_mix_g, ln_mix_b, ln_ffn_g, ln_ffn_b):
    bsz = x.shape[0]
    outs = []
    for bi in range(bsz):
        h = x[bi]
        for layer in range(DEPTH):
            i = layer // 2
            if layer % 2 == 0:
                h = _mixer_ab_layer(h, rel_table, ab_w_in[i], ab_sgu_ln_g[i], ab_sgu_ln_b[i], ab_sgu_w[i],
                                    ab_sgu_b[i], ab_cmp_pe_k[i], ab_cmp_w1_k[i], ab_cmp_w2_k[i],
                                    ab_cmp_pe_v[i], ab_cmp_w1_v[i], ab_cmp_w2_v[i], ab_w_out[i],
                                    ln_mix_g[layer], ln_mix_b[layer])
            else:
                h = _conformer(h, c_w_in[i].astype(jnp.bfloat16), _row(c_b_in[i]), c_dw_w[i], _row(c_dw_b[i]),
                               _row(c_norm_g[i]), _row(c_norm_b[i]), c_w_out[i].astype(jnp.bfloat16),
                               _row(ln_mix_g[layer]), _row(ln_mix_b[layer]))
            h = _conv_ffn(h, ffn_w_up[layer].astype(jnp.bfloat16), ffn_conv_w[layer], _row(ffn_conv_b[layer]),
                          ffn_w_down[layer].astype(jnp.bfloat16), _row(ln_ffn_g[layer]), _row(ln_ffn_b[layer]))
        outs.append(h)
    return jnp.stack(outs)
```

```python
import functools
import math

import numpy as np
import jax
import jax.numpy as jnp
from jax import lax
from jax.experimental import pallas as pl
from jax.experimental.pallas import tpu as pltpu

DEPTH = 2
A_GROUPS = 8
A_GROUP_DIM = 64
A_CHUNK = 128
B_HEADS = 8
B_HEAD_DIM = 64
B_KV_GROUPS = 2
B_HPG = B_HEADS // B_KV_GROUPS
CMP_BLOCK = 32
CMP_STRIDE = 16
SLC_BLOCK = 64
SLC_TOP_N = 16
WINDOW = 512
Q_BLOCK = 128
FORCED_SCORE = 1e4
REL_BUCKETS = 32
REL_MAX_DIST = 2048
LN_EPS = 1e-5
ALPHA = (2 * DEPTH) ** 0.25

A_WIDTH = A_GROUPS * A_GROUP_DIM
B_WIDTH = B_HEADS * B_HEAD_DIM
KV_WIDTH = B_KV_GROUPS * B_HEAD_DIM

LANES = 128
SUBLANES = 8
VMEM_LIMIT = 56 * 1024 * 1024

NEG = -1e30
M_INIT = -5e29
QL = B_HPG * Q_BLOCK
NL = B_KV_GROUPS * QL
KEY_TILE = 512
LANE_BLOCK = 256
SLC_PER_TILE = KEY_TILE // SLC_BLOCK
NEAR_BACK = 2048
NEAR_ROWS = NEAR_BACK + KEY_TILE
WIN_ROWS = WINDOW + Q_BLOCK
CMP_WIN = 128
CMP_SPECIAL = 16
SUB_ROWS = 32


def _bucket_starts():
    n = np.arange(2 * REL_MAX_DIST)
    max_exact = REL_BUCKETS // 2
    nf = np.maximum(n, max_exact).astype(np.float64)
    large = max_exact + np.floor(np.log(nf / max_exact) / math.log(REL_MAX_DIST / max_exact)
                                 * (REL_BUCKETS - max_exact)).astype(np.int64)
    bucket = np.where(n < max_exact, n, np.minimum(large, REL_BUCKETS - 1))
    return tuple(int(np.argmax(bucket >= b)) for b in range(REL_BUCKETS))


BUCKET_START = _bucket_starts()
FAR_DIST = BUCKET_START[-1]


def _gelu(x):
    c = math.sqrt(2.0 / math.pi)
    return x * (0.5 * (1.0 + jnp.tanh(c * (x + 0.044715 * (x * x * x)))))


def _layer_norm(x, g, b):
    mu = jnp.mean(x, axis=-1, keepdims=True)
    xc = x - mu
    var = jnp.mean(xc * xc, axis=-1, keepdims=True)
    return xc * lax.rsqrt(var + LN_EPS) * g + b


def _sigmoid(x):
    return 1.0 / (1.0 + jnp.exp(-x))


def _const_spec(shape):
    n = len(shape)
    return pl.BlockSpec(shape, lambda *_: (0,) * n, pipeline_mode=pl.Buffered(1))


def _params(sem):
    return pltpu.CompilerParams(dimension_semantics=sem, vmem_limit_bytes=VMEM_LIMIT)


def _proj_kernel(x_ref, w_ref, uv_ref, q_ref, cmp_ref, kv_ref, gate_ref):
    xb = x_ref[...].astype(jnp.bfloat16)
    o = 0
    for ref, scale in ((uv_ref, None), (q_ref, B_HEAD_DIM ** -0.5), (cmp_ref, None), (kv_ref, None),
                       (gate_ref, None)):
        n = ref.shape[1]
        h = jnp.dot(xb, w_ref[:, o:o + n], preferred_element_type=jnp.float32)
        if scale is not None:
            h = h * scale
        ref[...] = h.astype(ref.dtype)
        o += n


def _proj(x, w, tm=512):
    t, d = x.shape
    widths = (2 * A_WIDTH, B_HEADS * LANES, 2 * KV_WIDTH, 4 * KV_WIDTH, LANES)
    dtypes = (jnp.float32, jnp.bfloat16, jnp.float32, jnp.bfloat16, jnp.float32)
    assert w.shape == (d, sum(widths)) and t % tm == 0
    return pl.pallas_call(
        _proj_kernel,
        grid=(t // tm,),
        in_specs=[pl.BlockSpec((tm, d), lambda i: (i, 0)), _const_spec(w.shape)],
        out_specs=[pl.BlockSpec((tm, n), lambda i: (i, 0)) for n in widths],
        out_shape=[jax.ShapeDtypeStruct((t, n), dt) for n, dt in zip(widths, dtypes)],
        compiler_params=_params(("parallel",)),
        name="ab_in_proj",
    )(x, w)


def _sgu_kernel(u_ref, v_ref, g_ref, b_ref, ws_ref, bias_ref, o_ref):
    tm = u_ref.shape[0]
    u = _gelu(u_ref[...])
    v = _layer_norm(_gelu(v_ref[...]), g_ref[...], b_ref[...]).astype(jnp.bfloat16)
    lane_group = lax.broadcasted_iota(jnp.int32, (A_CHUNK, A_WIDTH), 1) // A_GROUP_DIM
    for c in range(tm // A_CHUNK):
        rows = slice(c * A_CHUNK, (c + 1) * A_CHUNK)
        vc = v[rows]
        s = bias_ref[...]
        for g in range(A_GROUPS):
            sg = jnp.dot(ws_ref[g], vc, preferred_element_type=jnp.float32)
            s = s + jnp.where(lane_group == g, sg, 0.0)
        o_ref[rows, :] = (u[rows] * s).astype(o_ref.dtype)


def _sgu(uv, ln_g, ln_b, ws, bias, tm=512):
    t = uv.shape[0]
    return pl.pallas_call(
        _sgu_kernel,
        grid=(t // tm,),
        in_specs=[pl.BlockSpec((tm, A_WIDTH), lambda i: (i, 0)),
                  pl.BlockSpec((tm, A_WIDTH), lambda i: (i, 1)),
                  _const_spec(ln_g.shape), _const_spec(ln_b.shape),
                  _const_spec(ws.shape), _const_spec(bias.shape)],
        out_specs=pl.BlockSpec((tm, A_WIDTH), lambda i: (i, 0)),
        out_shape=jax.ShapeDtypeStruct((t, A_WIDTH), jnp.bfloat16),
        compiler_params=_params(("parallel",)),
        name="sgu_mixer",
    )(uv, uv, ln_g, ln_b, ws, bias)


def _compress_kernel(r_ref, nxt_ref, pea_ref, peb_ref, wa_ref, wb_ref, w2_ref, o_ref):
    tm = r_ref.shape[0]
    r = r_ref[...]
    xa = (r + pea_ref[...]).astype(jnp.bfloat16)
    xb = (jnp.concatenate([r, nxt_ref[...]], axis=0) + peb_ref[...]).astype(jnp.bfloat16)
    ha = jnp.dot(xa, wa_ref[...], preferred_element_type=jnp.float32)
    hb = jnp.dot(xb, wb_ref[...], preferred_element_type=jnp.float32)
    hid = _gelu(ha + hb[1:tm + 1]).astype(jnp.bfloat16)
    o_ref[...] = jnp.dot(hid, w2_ref[...], preferred_element_type=jnp.float32)


def _compress(r, pea, peb, wa, wb, w2, tm=256):
    n, width = r.shape
    nblk = n // SUBLANES
    return pl.pallas_call(
        _compress_kernel,
        grid=(n // tm,),
        in_specs=[pl.BlockSpec((tm, width), lambda i: (i, 0)),
                  pl.BlockSpec((SUBLANES, width),
                               lambda i: (jnp.minimum((i + 1) * (tm // SUBLANES), nblk - 1), 0)),
                  _const_spec(pea.shape), _const_spec(peb.shape),
                  _const_spec(wa.shape), _const_spec(wb.shape), _const_spec(w2.shape)],
        out_specs=pl.BlockSpec((tm, w2.shape[1]), lambda i: (i, 0)),
        out_shape=jax.ShapeDtypeStruct((n, w2.shape[1]), jnp.float32),
        compiler_params=_params(("parallel",)),
        name="kv_compress",
    )(r, r, pea, peb, wa, wb, w2)


def _softmax_cols(l):
    m = jnp.maximum(jnp.max(l, axis=0, keepdims=True), M_INIT)
    e = jnp.exp(l - m)
    return e, jnp.sum(e, axis=0, keepdims=True)


def _nsa_kernel(q_ref, gate_ref, kc_ref, vct_ref, ks_ref, vst_ref, kw_ref, vwt_ref,
                bcmp_ref, bnear_ref, bwin_ref, o_ref,
                sc_ref, pcs_ref, sel_ref, acc_ref, l0_ref, l1_ref, p0_ref, p1_ref):
    qb = pl.program_id(0)
    s0 = qb * Q_BLOCK
    n_cmp = kc_ref.shape[0]
    n_slc = sel_ref.shape[1]

    q_all = jnp.concatenate([q_ref[:, h * LANES:(h + 1) * LANES] for h in range(B_HEADS)], axis=0)

    def scores(k):
        return lax.dot_general(k, q_all, (((1,), (1,)), ((), ())), preferred_element_type=jnp.float32)

    c0 = jnp.maximum(qb * (Q_BLOCK // CMP_STRIDE) - (CMP_WIN - Q_BLOCK // CMP_STRIDE), 0)
    c0 = pl.multiple_of(c0, SUBLANES)
    row_c = lax.broadcasted_iota(jnp.int32, (n_cmp, 1), 0)
    sc_ref[...] = jnp.where(row_c < c0 + CMP_WIN, scores(kc_ref[...]), NEG)
    sc_ref[pl.ds(c0, CMP_WIN), :] += bcmp_ref[0]
    e_c, sum_c = _softmax_cols(sc_ref[...])
    inv_c = 1.0 / jnp.maximum(sum_c, 1e-30)
    oc = jnp.dot(vct_ref[...], e_c.astype(jnp.bfloat16), preferred_element_type=jnp.float32)

    pc = e_c * inv_c
    t_row = s0 + lax.broadcasted_iota(jnp.int32, (1, Q_BLOCK), 1)
    blk = lax.broadcasted_iota(jnp.int32, (n_slc, Q_BLOCK), 0)
    blk_f = blk.astype(jnp.float32)
    own = blk == t_row // SLC_BLOCK
    valid = blk * SLC_BLOCK <= t_row
    ratio = SLC_BLOCK // CMP_STRIDE
    pcs_ref[0:SUBLANES, :] = jnp.zeros((SUBLANES, Q_BLOCK), jnp.float32)
    for g in range(B_KV_GROUPS):
        pg = pc[:, g * QL:(g + 1) * QL]
        pcs_ref[SUBLANES:, :] = sum(pg[:, p * Q_BLOCK:(p + 1) * Q_BLOCK] for p in range(B_HPG))
        score = pcs_ref[pl.ds(SUBLANES - 1, n_slc, stride=ratio), :]
        for r in range(ratio):
            score = score + pcs_ref[pl.ds(SUBLANES + r, n_slc, stride=ratio), :]
        score = jnp.where(valid, score, -1.0)
        score = jnp.where(own, FORCED_SCORE, jnp.where(blk == 0, FORCED_SCORE, score))
        sel = jnp.full((n_slc, Q_BLOCK), NEG, jnp.float32)
        for _ in range(min(SLC_TOP_N, n_slc)):
            best = jnp.max(score, axis=0, keepdims=True)
            first = jnp.min(jnp.where(score == best, blk_f, float(n_slc)), axis=0, keepdims=True)
            hit = blk_f == first
            sel = jnp.where(hit, 0.0, sel)
            score = jnp.where(hit, -2.0, score)
        sel_ref[g] = sel

    def fold8(v, op):
        return op(v.reshape(SLC_BLOCK // SUBLANES, SUBLANES, v.shape[1]), axis=0)

    m_far = jnp.maximum((s0 - (FAR_DIST - 1)) // KEY_TILE, 0)
    m_end = s0 // KEY_TILE + 1

    def key_rows(m):
        return pl.ds(pl.multiple_of(m * KEY_TILE, KEY_TILE), KEY_TILE)

    l_refs, p_refs = (l0_ref, l1_ref), (p0_ref, p1_ref)

    lane_blocks = [slice(n * LANE_BLOCK, (n + 1) * LANE_BLOCK) for n in range(NL // LANE_BLOCK)]

    def add_values(m, slot, alpha, lanes):
        pv = jnp.dot(vst_ref[:, key_rows(m)], p_refs[slot][:, lanes], preferred_element_type=jnp.float32)
        acc_ref[:, lanes] = alpha[:, lanes] * acc_ref[:, lanes] + pv

    def store_scores(m, slot, lanes):
        k = ks_ref[key_rows(m), :]
        l_refs[slot][:, lanes] = lax.dot_general(k, q_all[lanes], (((1,), (1,)), ((), ())),
                                                 preferred_element_type=jnp.float32)

    def tile_step(m, carry, near):
        return lax.cond(m % 2 == 0, functools.partial(tile_body, m, near=near, cur=0),
                        functools.partial(tile_body, m, near=near, cur=1), carry)

    def tile_body(m, carry, near, cur):
        m_run, s_run, alpha_prev = carry
        l_ref, p_ref, oth = l_refs[cur], p_refs[cur], 1 - cur
        m_next, m_prev = jnp.minimum(m + 1, m_end - 1), jnp.maximum(m - 1, 0)
        sel_rows = [sel_ref[g, pl.ds(pl.multiple_of(m * SLC_PER_TILE, SUBLANES), SLC_PER_TILE), :]
                    for g in range(B_KV_GROUPS)]
        if near:
            r0 = pl.multiple_of(m * KEY_TILE - s0 + NEAR_BACK, Q_BLOCK)
        outs = []
        for lanes in lane_blocks:
            store_scores(m_next, oth, lanes)
            add_values(m_prev, oth, alpha_prev, lanes)
            g = lanes.start // QL
            mx8 = jnp.full((SUBLANES, LANE_BLOCK), M_INIT, jnp.float32)
            for b in range(SLC_PER_TILE):
                rows = pl.ds(b * SLC_BLOCK, SLC_BLOCK)
                mrow = jnp.concatenate([sel_rows[g][b:b + 1, :]] * (LANE_BLOCK // Q_BLOCK), axis=1)
                l = l_ref[rows, lanes] + mrow
                if near:
                    l = l + bnear_ref[pl.ds(r0 + b * SLC_BLOCK, SLC_BLOCK), lanes]
                l_ref[rows, lanes] = l
                mx8 = jnp.maximum(mx8, fold8(l, jnp.max))
            m_new = jnp.maximum(m_run[:, lanes], jnp.max(mx8, axis=0, keepdims=True))
            alpha = jnp.exp(m_run[:, lanes] - m_new)
            s8 = jnp.zeros((SUBLANES, LANE_BLOCK), jnp.float32)
            for b in range(SLC_PER_TILE):
                rows = pl.ds(b * SLC_BLOCK, SLC_BLOCK)
                e = jnp.exp(l_ref[rows, lanes] - m_new)
                s8 = s8 + fold8(e, jnp.sum)
                p_ref[rows, lanes] = e.astype(jnp.bfloat16)
            outs.append((m_new, alpha * s_run[:, lanes] + jnp.sum(s8, axis=0, keepdims=True), alpha))
        return tuple(jnp.concatenate(parts, axis=1) for parts in zip(*outs))

    acc_ref[...] = jnp.zeros_like(acc_ref)
    p1_ref[...] = jnp.zeros_like(p1_ref)
    for lanes in lane_blocks:
        store_scores(0, 0, lanes)
    carry = (jnp.full((1, NL), M_INIT, jnp.float32), jnp.zeros((1, NL), jnp.float32),
             jnp.ones((1, NL), jnp.float32))
    carry = lax.fori_loop(0, m_far, functools.partial(tile_step, near=False), carry)
    _, sum_s, alpha_last = lax.fori_loop(m_far, m_end, functools.partial(tile_step, near=True), carry)

    for slot in range(2):
        @pl.when((m_end - 1) % 2 == slot)
        def _():
            for lanes in lane_blocks:
                add_values(m_end - 1, slot, alpha_last, lanes)

    osl = acc_ref[...]
    inv_s = 1.0 / jnp.maximum(sum_s, 1e-30)

    k0w = pl.multiple_of(s0, Q_BLOCK)
    row_w = lax.broadcasted_iota(jnp.int32, (WIN_ROWS, 1), 0)
    lw = scores(kw_ref[pl.ds(k0w, WIN_ROWS), :]) + bwin_ref[...]
    lw = jnp.where(row_w >= WINDOW - s0, lw, NEG)
    e_w, sum_w = _softmax_cols(lw)
    inv_w = 1.0 / jnp.maximum(sum_w, 1e-30)
    ow = jnp.dot(vwt_ref[:, pl.ds(k0w, WIN_ROWS)], e_w.astype(jnp.bfloat16),
                 preferred_element_type=jnp.float32)

    gt = _sigmoid(gate_ref[...]).T

    def gate_row(k):
        return jnp.concatenate([gt[k * B_HEADS + h:k * B_HEADS + h + 1, :] for h in range(B_HEADS)], axis=1)

    out_t = (oc * (gate_row(0) * inv_c) + osl * (gate_row(1) * inv_s) + ow * (gate_row(2) * inv_w))
    heads = []
    for h in range(B_HEADS):
        g = h // B_HPG
        blk_t = out_t[g * B_HEAD_DIM:(g + 1) * B_HEAD_DIM, h * Q_BLOCK:(h + 1) * Q_BLOCK]
        heads.append(blk_t.T)
    o_ref[...] = jnp.concatenate(heads, axis=1).astype(o_ref.dtype)


def _nsa(q, gates, kc, vct, ks, vst, kw, vwt, bcmp, bnear, bwin):
    t = q.shape[0]
    n_cmp = kc.shape[0]
    n_slc = t // SLC_BLOCK
    return pl.pallas_call(
        _nsa_kernel,
        grid=(t // Q_BLOCK,),
        in_specs=[pl.BlockSpec((Q_BLOCK, B_HEADS * LANES), lambda i: (i, 0)),
                  pl.BlockSpec((Q_BLOCK, LANES), lambda i: (i, 0)),
                  _const_spec(kc.shape), _const_spec(vct.shape),
                  _const_spec(ks.shape), _const_spec(vst.shape),
                  _const_spec(kw.shape), _const_spec(vwt.shape),
                  pl.BlockSpec((1, CMP_WIN, NL), lambda i: (jnp.minimum(i, CMP_SPECIAL - 1), 0, 0)),
                  _const_spec(bnear.shape), _const_spec(bwin.shape)],
        out_specs=pl.BlockSpec((Q_BLOCK, B_WIDTH), lambda i: (i, 0)),
        out_shape=jax.ShapeDtypeStruct((t, B_WIDTH), jnp.bfloat16),
        scratch_shapes=[pltpu.VMEM((n_cmp, NL), jnp.float32),
                        pltpu.VMEM((n_cmp + SUBLANES, Q_BLOCK), jnp.float32),
                        pltpu.VMEM((B_KV_GROUPS, n_slc, Q_BLOCK), jnp.float32),
                        pltpu.VMEM((LANES, NL), jnp.float32),
                        pltpu.VMEM((KEY_TILE, NL), jnp.float32),
                        pltpu.VMEM((KEY_TILE, NL), jnp.float32),
                        pltpu.VMEM((KEY_TILE, NL), jnp.bfloat16),
                        pltpu.VMEM((KEY_TILE, NL), jnp.bfloat16)],
        compiler_params=_params(("parallel",)),
        name="nsa_core",
    )(q, gates, kc, vct, ks, vst, kw, vwt, bcmp, bnear, bwin)


def _out_proj_kernel(a_ref, b_ref, x_ref, wa_ref, wb_ref, g_ref, beta_ref, o_ref):
    mix = jnp.dot(a_ref[...], wa_ref[...], preferred_element_type=jnp.float32)
    mix = mix + jnp.dot(b_ref[...], wb_ref[...], preferred_element_type=jnp.float32)
    o_ref[...] = _layer_norm(ALPHA * x_ref[...] + mix, g_ref[...], beta_ref[...])


def _out_proj(a, b, x, wa, wb, g, beta, tm=512):
    t, d = x.shape
    return pl.pallas_call(
        _out_proj_kernel,
        grid=(t // tm,),
        in_specs=[pl.BlockSpec((tm, a.shape[1]), lambda i: (i, 0)),
                  pl.BlockSpec((tm, b.shape[1]), lambda i: (i, 0)),
                  pl.BlockSpec((tm, d), lambda i: (i, 0)),
                  _const_spec(wa.shape), _const_spec(wb.shape),
                  _const_spec(g.shape), _const_spec(beta.shape)],
        out_specs=pl.BlockSpec((tm, d), lambda i: (i, 0)),
        out_shape=jax.ShapeDtypeStruct((t, d), jnp.float32),
        compiler_params=_params(("parallel",)),
        name="ab_out_proj_norm",
    )(a, b, x, wa, wb, g, beta)


FFN_CHUNK = 256


def _ffn_kernel(x_ref, wup_ref, cw_ref, cb_ref, wd_ref, g_ref, beta_ref,
                o_ref, xb_ref, act_ref, hbuf_ref, carry_ref):
    tm = x_ref.shape[0]
    width = cw_ref.shape[0]
    f = wd_ref.shape[0]

    @pl.when(pl.program_id(0) == 0)
    def _():
        carry_ref[...] = jnp.zeros_like(carry_ref)

    xb_ref[...] = x_ref[...].astype(jnp.bfloat16)
    for j in range(f // FFN_CHUNK):
        ys = []
        for part in range(2):
            cols = slice(part * f + j * FFN_CHUNK, part * f + (j + 1) * FFN_CHUNK)
            h = jnp.dot(xb_ref[...], wup_ref[:, cols], preferred_element_type=jnp.float32)
            hbuf_ref[part, 0:SUBLANES, :] = carry_ref[part, j]
            hbuf_ref[part, SUBLANES:, :] = h
            carry_ref[part, j] = hbuf_ref[part, tm:tm + SUBLANES, :]
            y = cb_ref[:, cols]
            for k in range(width):
                off = SUBLANES - (width - 1) + k
                y = y + cw_ref[k:k + 1, cols] * hbuf_ref[part, off:off + tm, :]
            ys.append(y)
        act_ref[:, j * FFN_CHUNK:(j + 1) * FFN_CHUNK] = (_gelu(ys[0]) * ys[1]).astype(jnp.bfloat16)
    mix = jnp.dot(act_ref[...], wd_ref[...], preferred_element_type=jnp.float32)
    o_ref[...] = _layer_norm(ALPHA * x_ref[...] + mix, g_ref[...], beta_ref[...])


def _conv_ffn(x, w_up, conv_w, conv_b, w_down, g, beta, tm=512):
    t, d = x.shape
    f = w_down.shape[0]
    assert f % FFN_CHUNK == 0 and t % tm == 0
    consts = (w_up, conv_w, conv_b, w_down, g, beta)
    return pl.pallas_call(
        _ffn_kernel,
        grid=(t // tm,),
        in_specs=[pl.BlockSpec((tm, d), lambda i: (i, 0))] + [_const_spec(c.shape) for c in consts],
        out_specs=pl.BlockSpec((tm, d), lambda i: (i, 0)),
        out_shape=jax.ShapeDtypeStruct((t, d), jnp.float32),
        scratch_shapes=[pltpu.VMEM((tm, d), jnp.bfloat16),
                        pltpu.VMEM((tm, f), jnp.bfloat16),
                        pltpu.VMEM((2, tm + SUBLANES, FFN_CHUNK), jnp.float32),
                        pltpu.VMEM((2, f // FFN_CHUNK, SUBLANES, FFN_CHUNK), jnp.float32)],
        compiler_params=_params(("arbitrary",)),
        name="conv_ffn_norm",
    )(x, *consts)


CONV_ROWS = 64
CONV_LANES = 256


def _conformer_kernel(x_ref, win_ref, bin_ref, dw_ref, dwb_ref, ng_ref, nb_ref, wout_ref, g_ref, beta_ref,
                      o_ref, gbuf_ref, ybuf_ref):
    i = pl.program_id(0)
    tm, d = x_ref.shape
    width = dw_ref.shape[0]
    halo = gbuf_ref.shape[0] - tm

    @pl.when(i == 0)
    def _():
        gbuf_ref[0:halo, :] = jnp.zeros((halo, d), jnp.float32)

    x = x_ref[...]
    h = jnp.dot(x.astype(jnp.bfloat16), win_ref[...], preferred_element_type=jnp.float32) + bin_ref[...]
    gbuf_ref[halo:, :] = h[:, :d] * _sigmoid(h[:, d:])

    base = halo - (width - 1)
    for r in range(0, tm, CONV_ROWS):
        for c in range(0, d, CONV_LANES):
            lanes = slice(c, c + CONV_LANES)
            y = jnp.broadcast_to(dwb_ref[:, lanes], (CONV_ROWS, CONV_LANES))
            for b in range(min(SUBLANES, width)):
                start, shift = divmod(base + b, SUBLANES)
                start = r + start * SUBLANES
                extra = SUBLANES if shift else 0
                part = None
                for k in range(b, width, SUBLANES):
                    rows = slice(start + k - b, start + k - b + CONV_ROWS + extra)
                    term = dw_ref[k:k + 1, lanes] * gbuf_ref[rows, lanes]
                    part = term if part is None else part + term
                y = y + part[shift:shift + CONV_ROWS]
            ybuf_ref[r:r + CONV_ROWS, lanes] = y
    gbuf_ref[0:halo, :] = gbuf_ref[tm:tm + halo, :]

    y = _layer_norm(ybuf_ref[...], ng_ref[...], nb_ref[...])
    y = (y * _sigmoid(y)).astype(jnp.bfloat16)
    mix = jnp.dot(y, wout_ref[...], preferred_element_type=jnp.float32)
    o_ref[...] = _layer_norm(ALPHA * x + mix, g_ref[...], beta_ref[...])


def _conformer(x, w_in, b_in, dw_w, dw_b, ng, nb, w_out, g, beta, tm=512):
    t, d = x.shape
    width = dw_w.shape[0]
    halo = -(-(width - 1) // SUBLANES) * SUBLANES
    consts = (w_in, b_in, dw_w, dw_b, ng, nb, w_out, g, beta)
    return pl.pallas_call(
        _conformer_kernel,
        grid=(t // tm,),
        in_specs=[pl.BlockSpec((tm, d), lambda i: (i, 0))] + [_const_spec(c.shape) for c in consts],
        out_specs=pl.BlockSpec((tm, d), lambda i: (i, 0)),
        out_shape=jax.ShapeDtypeStruct((t, d), jnp.float32),
        scratch_shapes=[pltpu.VMEM((tm + halo, d), jnp.float32), pltpu.VMEM((tm, d), jnp.float32)],
        compiler_params=_params(("arbitrary",)),
        name="conformer_conv_norm",
    )(x, *consts)


def _bias_kernel(tbl_ref, o_ref, *, kind):
    step = pl.program_id(0)
    rows = o_ref.shape[-2]
    out = o_ref if kind != "cmp" else o_ref.at[0]
    per_q = Q_BLOCK // CMP_STRIDE
    for sub in range(rows // SUB_ROWS):
        row = lax.broadcasted_iota(jnp.int32, (SUB_ROWS, Q_BLOCK), 0) + sub * SUB_ROWS
        lane = lax.broadcasted_iota(jnp.int32, (SUB_ROWS, Q_BLOCK), 1)
        if kind == "near":
            dist = lane + NEAR_BACK - (step * rows + row)
        elif kind == "win":
            dist = lane + WINDOW - (step * rows + row)
        else:
            c0 = jnp.maximum(step * per_q - (CMP_WIN - per_q), 0)
            dist = step * Q_BLOCK + lane - ((c0 + row) * CMP_STRIDE + CMP_BLOCK - 1)
        acc = [jnp.full((SUB_ROWS, Q_BLOCK), NEG, jnp.float32) for _ in range(B_HEADS)]
        for b in range(REL_BUCKETS):
            reached = dist >= BUCKET_START[b]
            for h in range(B_HEADS):
                acc[h] = jnp.where(reached, tbl_ref[b, h] - tbl_ref[REL_BUCKETS - 1, h], acc[h])
        for h in range(B_HEADS):
            if kind == "win":
                acc[h] = jnp.where(dist >= WINDOW, NEG, acc[h])
            out[sub * SUB_ROWS:(sub + 1) * SUB_ROWS, h * Q_BLOCK:(h + 1) * Q_BLOCK] = acc[h]


def _bias_table(rel_table, kind, shape, block):
    nd = len(shape)
    return pl.pallas_call(
        functools.partial(_bias_kernel, kind=kind),
        grid=(shape[0] // block[0],),
        in_specs=[pl.BlockSpec(memory_space=pltpu.SMEM)],
        out_specs=pl.BlockSpec(block, lambda i: (i,) + (0,) * (nd - 1)),
        out_shape=jax.ShapeDtypeStruct(shape, jnp.float32),
        compiler_params=_params(("parallel",)),
        name="rel_bias_" + kind,
    )(rel_table)


def _attention_tables(rel_table):
    bcmp = _bias_table(rel_table, "cmp", (CMP_SPECIAL, CMP_WIN, NL), (1, CMP_WIN, NL))
    bnear = _bias_table(rel_table, "near", (NEAR_ROWS, NL), (Q_BLOCK, NL))
    bwin = _bias_table(rel_table, "win", (WIN_ROWS, NL), (Q_BLOCK, NL))
    return bcmp, bnear, bwin


def _ab_weights(w_in, w_out):
    d = w_in.shape[0]
    o = 2 * A_WIDTH
    w_uv = w_in[:, :o]
    w_q = w_in[:, o:o + B_WIDTH].reshape(d, B_HEADS, B_HEAD_DIM)
    o += B_WIDTH
    q_pad = jnp.zeros((d, B_HEADS, B_KV_GROUPS, B_HEAD_DIM), w_in.dtype)
    for h in range(B_HEADS):
        q_pad = q_pad.at[:, h, h // B_HPG].set(w_q[:, h])
    w_kv = w_in[:, o:o + 6 * KV_WIDTH]
    o += 6 * KV_WIDTH
    w_g = w_in[:, o:].reshape(d, B_HEADS, 3).transpose(0, 2, 1).reshape(d, 3 * B_HEADS)
    w_g = jnp.pad(w_g, ((0, 0), (0, LANES - 3 * B_HEADS)))
    w = jnp.concatenate([w_uv, q_pad.reshape(d, B_HEADS * LANES), w_kv, w_g], axis=1)
    return w.astype(jnp.bfloat16), w_out[:A_WIDTH].astype(jnp.bfloat16), w_out[A_WIDTH:].astype(jnp.bfloat16)


def _compress_weights(pe_k, w1_k, w2_k, pe_v, w1_v, w2_v):
    hid = w1_k.shape[1]
    nseg = 2 * B_KV_GROUPS
    halves = []
    pes = []
    for half in range(CMP_BLOCK // CMP_STRIDE):
        rows = slice(half * CMP_STRIDE, (half + 1) * CMP_STRIDE)
        w = jnp.zeros((CMP_STRIDE, nseg, B_HEAD_DIM, nseg, hid), jnp.float32)
        pe = jnp.zeros((CMP_STRIDE, nseg, B_HEAD_DIM), jnp.float32)
        for seg in range(nseg):
            w1, pe_src = (w1_k, pe_k) if seg < B_KV_GROUPS else (w1_v, pe_v)
            w = w.at[:, seg, :, seg, :].set(w1.reshape(CMP_BLOCK, B_HEAD_DIM, hid)[rows])
            pe = pe.at[:, seg, :].set(pe_src[rows])
        halves.append(w.reshape(CMP_STRIDE * nseg * B_HEAD_DIM, nseg * hid).astype(jnp.bfloat16))
        pes.append(pe.reshape(1, CMP_STRIDE * nseg * B_HEAD_DIM))
    w2 = jnp.zeros((nseg, hid, nseg, B_HEAD_DIM), jnp.float32)
    for seg in range(nseg):
        w2 = w2.at[seg, :, seg, :].set(w2_k if seg < B_KV_GROUPS else w2_v)
    return pes[0], pes[1], halves[0], halves[1], w2.reshape(nseg * hid, nseg * B_HEAD_DIM).astype(jnp.bfloat16)


def _row(v):
    return v.reshape(1, -1)


def _mixer_ab_layer(x, rel_table, w_in, sgu_ln_g, sgu_ln_b, sgu_w, sgu_b,
                    pe_k, w1_k, w2_k, pe_v, w1_v, w2_v, w_out, ln_g, ln_b):
    t = x.shape[0]
    w_all, w_out_a, w_out_b = _ab_weights(w_in, w_out)
    uv, q, cmp_in, kv, gates = _proj(x, w_all)

    causal = jnp.tril(jnp.ones((A_CHUNK, A_CHUNK), dtype=bool))
    ws = jnp.where(causal[None], sgu_w, 0.0).astype(jnp.bfloat16)
    sgu_bias = jnp.repeat(sgu_b.T, A_GROUP_DIM, axis=1)
    a_out = _sgu(uv, _row(sgu_ln_g), _row(sgu_ln_b), ws, sgu_bias)

    pea, peb, wa, wb, w2 = _compress_weights(pe_k, w1_k, w2_k, pe_v, w1_v, w2_v)
    cmp_out = _compress(cmp_in.reshape(t // CMP_STRIDE, CMP_STRIDE * 2 * KV_WIDTH), pea, peb, wa, wb, w2)
    kc = cmp_out[:, :KV_WIDTH].astype(jnp.bfloat16)
    vct = cmp_out[:, KV_WIDTH:].T.astype(jnp.bfloat16)

    ks, vs, kw, vw = (kv[:, n * KV_WIDTH:(n + 1) * KV_WIDTH] for n in range(4))
    kw_pad = jnp.pad(kw, ((WINDOW, 0), (0, 0)))
    vwt_pad = jnp.pad(vw, ((WINDOW, 0), (0, 0))).T
    bcmp, bnear, bwin = _attention_tables(rel_table)
    b_out = _nsa(q, gates, kc, vct, ks, vs.T, kw_pad, vwt_pad, bcmp, bnear, bwin)

    return _out_proj(a_out, b_out, x, w_out_a, w_out_b, _row(ln_g), _row(ln_b))


def kernel(x, rel_table, ab_w_in, ab_sgu_ln_g, ab_sgu_ln_b, ab_sgu_w, ab_sgu_b, ab_cmp_pe_k, ab_cmp_w1_k, ab_cmp_w2_k, ab_cmp_pe_v, ab_cmp_w1_v, ab_cmp_w2_v, ab_w_out, c_w_in, c_b_in, c_dw_w, c_dw_b, c_norm_g, c_norm_b, c_w_out, ffn_w_up, ffn_conv_w, ffn_conv_b, ffn_w_down, ln_mix_g, ln_mix_b, ln_ffn_g, ln_ffn_b):
    bsz = x.shape[0]
    outs = []
    for bi in range(bsz):
        h = x[bi]
        for layer in range(DEPTH):
            i = layer // 2
            if layer % 2 == 0:
                h = _mixer_ab_layer(h, rel_table, ab_w_in[i], ab_sgu_ln_g[i], ab_sgu_ln_b[i], ab_sgu_w[i],
                                    ab_sgu_b[i], ab_cmp_pe_k[i], ab_cmp_w1_k[i], ab_cmp_w2_k[i],
                                    ab_cmp_pe_v[i], ab_cmp_w1_v[i], ab_cmp_w2_v[i], ab_w_out[i],
                                    ln_mix_g[layer], ln_mix_b[layer])
            else:
                h = _conformer(h, c_w_in[i].astype(jnp.bfloat16), _row(c_b_in[i]), c_dw_w[i], _row(c_dw_b[i]),
                               _row(c_norm_g[i]), _row(c_norm_b[i]), c_w_out[i].astype(jnp.bfloat16),
                               _row(ln_mix_g[layer]), _row(ln_mix_b[layer]))
            h = _conv_ffn(h, ffn_w_up[layer].astype(jnp.bfloat16), ffn_conv_w[layer], _row(ffn_conv_b[layer]),
                          ffn_w_down[layer].astype(jnp.bfloat16), _row(ln_ffn_g[layer]), _row(ln_ffn_b[layer]))
        outs.append(h)
    return jnp.stack(outs)
```

```python
import functools
import math

import numpy as np
import jax
import jax.numpy as jnp
from jax import lax
from jax.experimental import pallas as pl
from jax.experimental.pallas import tpu as pltpu

DEPTH = 2
A_GROUPS = 8
A_GROUP_DIM = 64
A_CHUNK = 128
B_HEADS = 8
B_HEAD_DIM = 64
B_KV_GROUPS = 2
B_HPG = B_HEADS // B_KV_GROUPS
CMP_BLOCK = 32
CMP_STRIDE = 16
SLC_BLOCK = 64
SLC_TOP_N = 16
WINDOW = 512
Q_BLOCK = 128
FORCED_SCORE = 1e4
REL_BUCKETS = 32
REL_MAX_DIST = 2048
LN_EPS = 1e-5
ALPHA = (2 * DEPTH) ** 0.25

A_WIDTH = A_GROUPS * A_GROUP_DIM
B_WIDTH = B_HEADS * B_HEAD_DIM
KV_WIDTH = B_KV_GROUPS * B_HEAD_DIM

LANES = 128
SUBLANES = 8
VMEM_LIMIT = 56 * 1024 * 1024

LOG2E = math.log2(math.e)
NEG = -1e30
M_INIT = -5e29
QL = B_HPG * Q_BLOCK
NL = B_KV_GROUPS * QL
KEY_TILE = 512
LANE_BLOCK = 256
SLC_PER_TILE = KEY_TILE // SLC_BLOCK
NEAR_BACK = 2048
NEAR_ROWS = NEAR_BACK + KEY_TILE
WIN_ROWS = WINDOW + Q_BLOCK
CMP_WIN = 128
CMP_SPECIAL = 16
SUB_ROWS = 32


def _bucket_starts():
    n = np.arange(2 * REL_MAX_DIST)
    max_exact = REL_BUCKETS // 2
    nf = np.maximum(n, max_exact).astype(np.float64)
    large = max_exact + np.floor(np.log(nf / max_exact) / math.log(REL_MAX_DIST / max_exact)
                                 * (REL_BUCKETS - max_exact)).astype(np.int64)
    bucket = np.where(n < max_exact, n, np.minimum(large, REL_BUCKETS - 1))
    return tuple(int(np.argmax(bucket >= b)) for b in range(REL_BUCKETS))


BUCKET_START = _bucket_starts()
FAR_DIST = BUCKET_START[-1]


def _gelu(x):
    c = math.sqrt(2.0 / math.pi)
    return x * (0.5 * (1.0 + jnp.tanh(c * (x + 0.044715 * (x * x * x)))))


def _layer_norm(x, g, b):
    mu = jnp.mean(x, axis=-1, keepdims=True)
    xc = x - mu
    var = jnp.mean(xc * xc, axis=-1, keepdims=True)
    return xc * lax.rsqrt(var + LN_EPS) * g + b


def _sigmoid(x):
    return 1.0 / (1.0 + jnp.exp(-x))


def _const_spec(shape):
    n = len(shape)
    return pl.BlockSpec(shape, lambda *_: (0,) * n, pipeline_mode=pl.Buffered(1))


def _params(sem):
    return pltpu.CompilerParams(dimension_semantics=sem, vmem_limit_bytes=VMEM_LIMIT)


def _proj_kernel(x_ref, w_ref, uv_ref, q_ref, cmp_ref, kv_ref, gate_ref):
    xb = x_ref[...].astype(jnp.bfloat16)
    o = 0
    for ref, scale in ((uv_ref, None), (q_ref, B_HEAD_DIM ** -0.5 * LOG2E), (cmp_ref, None), (kv_ref, None),
                       (gate_ref, None)):
        n = ref.shape[1]
        h = jnp.dot(xb, w_ref[:, o:o + n], preferred_element_type=jnp.float32)
        if scale is not None:
            h = h * scale
        ref[...] = h.astype(ref.dtype)
        o += n


def _proj(x, w, tm=512):
    t, d = x.shape
    widths = (2 * A_WIDTH, B_HEADS * LANES, 2 * KV_WIDTH, 4 * KV_WIDTH, LANES)
    dtypes = (jnp.float32, jnp.bfloat16, jnp.float32, jnp.bfloat16, jnp.float32)
    assert w.shape == (d, sum(widths)) and t % tm == 0
    return pl.pallas_call(
        _proj_kernel,
        grid=(t // tm,),
        in_specs=[pl.BlockSpec((tm, d), lambda i: (i, 0)), _const_spec(w.shape)],
        out_specs=[pl.BlockSpec((tm, n), lambda i: (i, 0)) for n in widths],
        out_shape=[jax.ShapeDtypeStruct((t, n), dt) for n, dt in zip(widths, dtypes)],
        compiler_params=_params(("parallel",)),
        name="ab_in_proj",
    )(x, w)


def _sgu_kernel(u_ref, v_ref, g_ref, b_ref, ws_ref, bias_ref, o_ref):
    tm = u_ref.shape[0]
    u = _gelu(u_ref[...])
    v = _layer_norm(_gelu(v_ref[...]), g_ref[...], b_ref[...]).astype(jnp.bfloat16)
    lane_group = lax.broadcasted_iota(jnp.int32, (A_CHUNK, A_WIDTH), 1) // A_GROUP_DIM
    for c in range(tm // A_CHUNK):
        rows = slice(c * A_CHUNK, (c + 1) * A_CHUNK)
        vc = v[rows]
        s = bias_ref[...]
        for g in range(A_GROUPS):
            sg = jnp.dot(ws_ref[g], vc, preferred_element_type=jnp.float32)
            s = s + jnp.where(lane_group == g, sg, 0.0)
        o_ref[rows, :] = (u[rows] * s).astype(o_ref.dtype)


def _sgu(uv, ln_g, ln_b, ws, bias, tm=512):
    t = uv.shape[0]
    return pl.pallas_call(
        _sgu_kernel,
        grid=(t // tm,),
        in_specs=[pl.BlockSpec((tm, A_WIDTH), lambda i: (i, 0)),
                  pl.BlockSpec((tm, A_WIDTH), lambda i: (i, 1)),
                  _const_spec(ln_g.shape), _const_spec(ln_b.shape),
                  _const_spec(ws.shape), _const_spec(bias.shape)],
        out_specs=pl.BlockSpec((tm, A_WIDTH), lambda i: (i, 0)),
        out_shape=jax.ShapeDtypeStruct((t, A_WIDTH), jnp.bfloat16),
        compiler_params=_params(("parallel",)),
        name="sgu_mixer",
    )(uv, uv, ln_g, ln_b, ws, bias)


def _compress_kernel(r_ref, nxt_ref, pea_ref, peb_ref, wa_ref, wb_ref, w2_ref, o_ref):
    tm = r_ref.shape[0]
    r = r_ref[...]
    xa = (r + pea_ref[...]).astype(jnp.bfloat16)
    xb = (jnp.concatenate([r, nxt_ref[...]], axis=0) + peb_ref[...]).astype(jnp.bfloat16)
    ha = jnp.dot(xa, wa_ref[...], preferred_element_type=jnp.float32)
    hb = jnp.dot(xb, wb_ref[...], preferred_element_type=jnp.float32)
    hid = _gelu(ha + hb[1:tm + 1]).astype(jnp.bfloat16)
    o_ref[...] = jnp.dot(hid, w2_ref[...], preferred_element_type=jnp.float32)


def _compress(r, pea, peb, wa, wb, w2, tm=256):
    n, width = r.shape
    nblk = n // SUBLANES
    return pl.pallas_call(
        _compress_kernel,
        grid=(n // tm,),
        in_specs=[pl.BlockSpec((tm, width), lambda i: (i, 0)),
                  pl.BlockSpec((SUBLANES, width),
                               lambda i: (jnp.minimum((i + 1) * (tm // SUBLANES), nblk - 1), 0)),
                  _const_spec(pea.shape), _const_spec(peb.shape),
                  _const_spec(wa.shape), _const_spec(wb.shape), _const_spec(w2.shape)],
        out_specs=pl.BlockSpec((tm, w2.shape[1]), lambda i: (i, 0)),
        out_shape=jax.ShapeDtypeStruct((n, w2.shape[1]), jnp.float32),
        compiler_params=_params(("parallel",)),
        name="kv_compress",
    )(r, r, pea, peb, wa, wb, w2)


def _softmax_cols(l):
    m = jnp.maximum(jnp.max(l, axis=0, keepdims=True), M_INIT)
    e = jnp.exp2(l - m)
    return e, jnp.sum(e, axis=0, keepdims=True)


def _nsa_kernel(q_ref, gate_ref, kc_ref, vct_ref, ks_ref, vst_ref, kw_ref, vwt_ref,
                bcmp_ref, bnear_ref, bwin_ref, o_ref,
                sc_ref, pcs_ref, sel_ref, acc_ref, l0_ref, l1_ref, p0_ref, p1_ref):
    qb = pl.program_id(0)
    s0 = qb * Q_BLOCK
    n_cmp = kc_ref.shape[0]
    n_slc = sel_ref.shape[1]

    q_t = jnp.concatenate([q_ref[:, h * LANES:(h + 1) * LANES].T for h in range(B_HEADS)], axis=1)

    def scores(k):
        return jnp.dot(k, q_t, preferred_element_type=jnp.float32)

    c0 = jnp.maximum(qb * (Q_BLOCK // CMP_STRIDE) - (CMP_WIN - Q_BLOCK // CMP_STRIDE), 0)
    c0 = pl.multiple_of(c0, SUBLANES)
    row_c = lax.broadcasted_iota(jnp.int32, (n_cmp, 1), 0)
    sc_ref[...] = jnp.where(row_c < c0 + CMP_WIN, scores(kc_ref[...]), NEG)
    sc_ref[pl.ds(c0, CMP_WIN), :] += bcmp_ref[0]
    e_c, sum_c = _softmax_cols(sc_ref[...])
    inv_c = 1.0 / jnp.maximum(sum_c, 1e-30)
    oc = jnp.dot(vct_ref[...], e_c.astype(jnp.bfloat16), preferred_element_type=jnp.float32)

    pc = e_c * inv_c
    t_row = s0 + lax.broadcasted_iota(jnp.int32, (1, Q_BLOCK), 1)
    blk = lax.broadcasted_iota(jnp.int32, (n_slc, Q_BLOCK), 0)
    blk_f = blk.astype(jnp.float32)
    own = blk == t_row // SLC_BLOCK
    valid = blk * SLC_BLOCK <= t_row
    ratio = SLC_BLOCK // CMP_STRIDE
    pcs_ref[0:SUBLANES, :] = jnp.zeros((SUBLANES, Q_BLOCK), jnp.float32)
    for g in range(B_KV_GROUPS):
        pg = pc[:, g * QL:(g + 1) * QL]
        pcs_ref[SUBLANES:, :] = sum(pg[:, p * Q_BLOCK:(p + 1) * Q_BLOCK] for p in range(B_HPG))
        score = pcs_ref[pl.ds(SUBLANES - 1, n_slc, stride=ratio), :]
        for r in range(ratio):
            score = score + pcs_ref[pl.ds(SUBLANES + r, n_slc, stride=ratio), :]
        score = jnp.where(valid, score, -1.0)
        score = jnp.where(own, FORCED_SCORE, jnp.where(blk == 0, FORCED_SCORE, score))
        sel = jnp.full((n_slc, Q_BLOCK), NEG, jnp.float32)
        for _ in range(min(SLC_TOP_N, n_slc)):
            best = jnp.max(score, axis=0, keepdims=True)
            first = jnp.min(jnp.where(score == best, blk_f, float(n_slc)), axis=0, keepdims=True)
            hit = blk_f == first
            sel = jnp.where(hit, 0.0, sel)
            score = jnp.where(hit, -2.0, score)
        sel_ref[g] = sel

    def fold8(v, op):
        return op(v.reshape(SLC_BLOCK // SUBLANES, SUBLANES, v.shape[1]), axis=0)

    m_far = jnp.maximum((s0 - (FAR_DIST - 1)) // KEY_TILE, 0)
    m_end = s0 // KEY_TILE + 1

    def key_rows(m):
        return pl.ds(pl.multiple_of(m * KEY_TILE, KEY_TILE), KEY_TILE)

    l_refs, p_refs = (l0_ref, l1_ref), (p0_ref, p1_ref)

    lane_blocks = [slice(n * LANE_BLOCK, (n + 1) * LANE_BLOCK) for n in range(NL // LANE_BLOCK)]

    def add_values(m, slot, alpha, lanes):
        pv = jnp.dot(vst_ref[:, key_rows(m)], p_refs[slot][:, lanes], preferred_element_type=jnp.float32)
        acc_ref[:, lanes] = alpha[:, lanes] * acc_ref[:, lanes] + pv

    def store_scores(m, slot, lanes):
        k = ks_ref[key_rows(m), :]
        l_refs[slot][:, lanes] = jnp.dot(k, q_t[:, lanes], preferred_element_type=jnp.float32)

    def tile_step(m, carry, near):
        return lax.cond(m % 2 == 0, functools.partial(tile_body, m, near=near, cur=0),
                        functools.partial(tile_body, m, near=near, cur=1), carry)

    def tile_body(m, carry, near, cur):
        m_run, s_run, alpha_prev = carry
        l_ref, p_ref, oth = l_refs[cur], p_refs[cur], 1 - cur
        m_next, m_prev = jnp.minimum(m + 1, m_end - 1), jnp.maximum(m - 1, 0)
        sel_rows = [sel_ref[g, pl.ds(pl.multiple_of(m * SLC_PER_TILE, SUBLANES), SLC_PER_TILE), :]
                    for g in range(B_KV_GROUPS)]
        if near:
            r0 = pl.multiple_of(m * KEY_TILE - s0 + NEAR_BACK, Q_BLOCK)
        outs = []
        for lanes in lane_blocks:
            store_scores(m_next, oth, lanes)
            add_values(m_prev, oth, alpha_prev, lanes)
            g = lanes.start // QL
            mrows = [jnp.concatenate([sel_rows[g][b:b + 1, :]] * (LANE_BLOCK // Q_BLOCK), axis=1)
                     for b in range(SLC_PER_TILE)]
            mx8 = jnp.full((SUBLANES, LANE_BLOCK), M_INIT, jnp.float32)
            for b in range(SLC_PER_TILE):
                rows = pl.ds(b * SLC_BLOCK, SLC_BLOCK)
                if near:
                    l = l_ref[rows, lanes] + bnear_ref[pl.ds(r0 + b * SLC_BLOCK, SLC_BLOCK), lanes]
                    l_ref[rows, lanes] = l
                else:
                    l = l_ref[rows, lanes]
                mx8 = jnp.maximum(mx8, fold8(l + mrows[b], jnp.max))
            m_new = jnp.maximum(m_run[:, lanes], jnp.max(mx8, axis=0, keepdims=True))
            alpha = jnp.exp2(m_run[:, lanes] - m_new)
            s8 = jnp.zeros((SUBLANES, LANE_BLOCK), jnp.float32)
            for b in range(SLC_PER_TILE):
                rows = pl.ds(b * SLC_BLOCK, SLC_BLOCK)
                e = jnp.exp2(l_ref[rows, lanes] + (mrows[b] - m_new))
                s8 = s8 + fold8(e, jnp.sum)
                p_ref[rows, lanes] = e.astype(jnp.bfloat16)
            outs.append((m_new, alpha * s_run[:, lanes] + jnp.sum(s8, axis=0, keepdims=True), alpha))
        return tuple(jnp.concatenate(parts, axis=1) for parts in zip(*outs))

    acc_ref[...] = jnp.zeros_like(acc_ref)
    p1_ref[...] = jnp.zeros_like(p1_ref)
    for lanes in lane_blocks:
        store_scores(0, 0, lanes)
    carry = (jnp.full((1, NL), M_INIT, jnp.float32), jnp.zeros((1, NL), jnp.float32),
             jnp.ones((1, NL), jnp.float32))
    carry = lax.fori_loop(0, m_far, functools.partial(tile_step, near=False), carry)
    _, sum_s, alpha_last = lax.fori_loop(m_far, m_end, functools.partial(tile_step, near=True), carry)

    for slot in range(2):
        @pl.when((m_end - 1) % 2 == slot)
        def _():
            for lanes in lane_blocks:
                add_values(m_end - 1, slot, alpha_last, lanes)

    osl = acc_ref[...]
    inv_s = 1.0 / jnp.maximum(sum_s, 1e-30)

    k0w = pl.multiple_of(s0, Q_BLOCK)
    row_w = lax.broadcasted_iota(jnp.int32, (WIN_ROWS, 1), 0)
    lw = scores(kw_ref[pl.ds(k0w, WIN_ROWS), :]) + bwin_ref[...]
    lw = jnp.where(row_w >= WINDOW - s0, lw, NEG)
    e_w, sum_w = _softmax_cols(lw)
    inv_w = 1.0 / jnp.maximum(sum_w, 1e-30)
    ow = jnp.dot(vwt_ref[:, pl.ds(k0w, WIN_ROWS)], e_w.astype(jnp.bfloat16),
                 preferred_element_type=jnp.float32)

    gt = _sigmoid(gate_ref[...]).T

    def gate_row(k):
        return jnp.concatenate([gt[k * B_HEADS + h:k * B_HEADS + h + 1, :] for h in range(B_HEADS)], axis=1)

    out_t = (oc * (gate_row(0) * inv_c) + osl * (gate_row(1) * inv_s) + ow * (gate_row(2) * inv_w))
    heads = []
    for h in range(B_HEADS):
        g = h // B_HPG
        blk_t = out_t[g * B_HEAD_DIM:(g + 1) * B_HEAD_DIM, h * Q_BLOCK:(h + 1) * Q_BLOCK]
        heads.append(blk_t.T)
    o_ref[...] = jnp.concatenate(heads, axis=1).astype(o_ref.dtype)


def _nsa(q, gates, kc, vct, ks, vst, kw, vwt, bcmp, bnear, bwin):
    t = q.shape[0]
    n_cmp = kc.shape[0]
    n_slc = t // SLC_BLOCK
    return pl.pallas_call(
        _nsa_kernel,
        grid=(t // Q_BLOCK,),
        in_specs=[pl.BlockSpec((Q_BLOCK, B_HEADS * LANES), lambda i: (i, 0)),
                  pl.BlockSpec((Q_BLOCK, LANES), lambda i: (i, 0)),
                  _const_spec(kc.shape), _const_spec(vct.shape),
                  _const_spec(ks.shape), _const_spec(vst.shape),
                  _const_spec(kw.shape), _const_spec(vwt.shape),
                  pl.BlockSpec((1, CMP_WIN, NL), lambda i: (jnp.minimum(i, CMP_SPECIAL - 1), 0, 0)),
                  _const_spec(bnear.shape), _const_spec(bwin.shape)],
        out_specs=pl.BlockSpec((Q_BLOCK, B_WIDTH), lambda i: (i, 0)),
        out_shape=jax.ShapeDtypeStruct((t, B_WIDTH), jnp.bfloat16),
        scratch_shapes=[pltpu.VMEM((n_cmp, NL), jnp.float32),
                        pltpu.VMEM((n_cmp + SUBLANES, Q_BLOCK), jnp.float32),
                        pltpu.VMEM((B_KV_GROUPS, n_slc, Q_BLOCK), jnp.float32),
                        pltpu.VMEM((LANES, NL), jnp.float32),
                        pltpu.VMEM((KEY_TILE, NL), jnp.float32),
                        pltpu.VMEM((KEY_TILE, NL), jnp.float32),
                        pltpu.VMEM((KEY_TILE, NL), jnp.bfloat16),
                        pltpu.VMEM((KEY_TILE, NL), jnp.bfloat16)],
        compiler_params=_params(("parallel",)),
        name="nsa_core",
    )(q, gates, kc, vct, ks, vst, kw, vwt, bcmp, bnear, bwin)


def _out_proj_kernel(a_ref, b_ref, x_ref, wa_ref, wb_ref, g_ref, beta_ref, o_ref):
    mix = jnp.dot(a_ref[...], wa_ref[...], preferred_element_type=jnp.float32)
    mix = mix + jnp.dot(b_ref[...], wb_ref[...], preferred_element_type=jnp.float32)
    o_ref[...] = _layer_norm(ALPHA * x_ref[...] + mix, g_ref[...], beta_ref[...])


def _out_proj(a, b, x, wa, wb, g, beta, tm=512):
    t, d = x.shape
    return pl.pallas_call(
        _out_proj_kernel,
        grid=(t // tm,),
        in_specs=[pl.BlockSpec((tm, a.shape[1]), lambda i: (i, 0)),
                  pl.BlockSpec((tm, b.shape[1]), lambda i: (i, 0)),
                  pl.BlockSpec((tm, d), lambda i: (i, 0)),
                  _const_spec(wa.shape), _const_spec(wb.shape),
                  _const_spec(g.shape), _const_spec(beta.shape)],
        out_specs=pl.BlockSpec((tm, d), lambda i: (i, 0)),
        out_shape=jax.ShapeDtypeStruct((t, d), jnp.float32),
        compiler_params=_params(("parallel",)),
        name="ab_out_proj_norm",
    )(a, b, x, wa, wb, g, beta)


FFN_CHUNK = 256


def _ffn_kernel(x_ref, wup_ref, cw_ref, cb_ref, wd_ref, g_ref, beta_ref,
                o_ref, xb_ref, act_ref, hbuf_ref, carry_ref):
    tm = x_ref.shape[0]
    width = cw_ref.shape[0]
    f = wd_ref.shape[0]

    @pl.when(pl.program_id(0) == 0)
    def _():
        carry_ref[...] = jnp.zeros_like(carry_ref)

    xb_ref[...] = x_ref[...].astype(jnp.bfloat16)
    for j in range(f // FFN_CHUNK):
        ys = []
        for part in range(2):
            cols = slice(part * f + j * FFN_CHUNK, part * f + (j + 1) * FFN_CHUNK)
            h = jnp.dot(xb_ref[...], wup_ref[:, cols], preferred_element_type=jnp.float32)
            hbuf_ref[part, 0:SUBLANES, :] = carry_ref[part, j]
            hbuf_ref[part, SUBLANES:, :] = h
            carry_ref[part, j] = hbuf_ref[part, tm:tm + SUBLANES, :]
            y = cb_ref[:, cols]
            for k in range(width):
                off = SUBLANES - (width - 1) + k
                y = y + cw_ref[k:k + 1, cols] * hbuf_ref[part, off:off + tm, :]
            ys.append(y)
        act_ref[:, j * FFN_CHUNK:(j + 1) * FFN_CHUNK] = (_gelu(ys[0]) * ys[1]).astype(jnp.bfloat16)
    mix = jnp.dot(act_ref[...], wd_ref[...], preferred_element_type=jnp.float32)
    o_ref[...] = _layer_norm(ALPHA * x_ref[...] + mix, g_ref[...], beta_ref[...])


def _conv_ffn(x, w_up, conv_w, conv_b, w_down, g, beta, tm=512):
    t, d = x.shape
    f = w_down.shape[0]
    assert f % FFN_CHUNK == 0 and t % tm == 0
    consts = (w_up, conv_w, conv_b, w_down, g, beta)
    return pl.pallas_call(
        _ffn_kernel,
        grid=(t // tm,),
        in_specs=[pl.BlockSpec((tm, d), lambda i: (i, 0))] + [_const_spec(c.shape) for c in consts],
        out_specs=pl.BlockSpec((tm, d), lambda i: (i, 0)),
        out_shape=jax.ShapeDtypeStruct((t, d), jnp.float32),
        scratch_shapes=[pltpu.VMEM((tm, d), jnp.bfloat16),
                        pltpu.VMEM((tm, f), jnp.bfloat16),
                        pltpu.VMEM((2, tm + SUBLANES, FFN_CHUNK), jnp.float32),
                        pltpu.VMEM((2, f // FFN_CHUNK, SUBLANES, FFN_CHUNK), jnp.float32)],
        compiler_params=_params(("arbitrary",)),
        name="conv_ffn_norm",
    )(x, *consts)


CONV_ROWS = 64
CONV_LANES = 256


def _conformer_kernel(x_ref, win_ref, bin_ref, dw_ref, dwb_ref, ng_ref, nb_ref, wout_ref, g_ref, beta_ref,
                      o_ref, gbuf_ref, ybuf_ref):
    i = pl.program_id(0)
    tm, d = x_ref.shape
    width = dw_ref.shape[0]
    halo = gbuf_ref.shape[0] - tm

    @pl.when(i == 0)
    def _():
        gbuf_ref[0:halo, :] = jnp.zeros((halo, d), jnp.float32)

    x = x_ref[...]
    h = jnp.dot(x.astype(jnp.bfloat16), win_ref[...], preferred_element_type=jnp.float32) + bin_ref[...]
    gbuf_ref[halo:, :] = h[:, :d] * _sigmoid(h[:, d:])

    base = halo - (width - 1)
    for r in range(0, tm, CONV_ROWS):
        for c in range(0, d, CONV_LANES):
            lanes = slice(c, c + CONV_LANES)
            y = jnp.broadcast_to(dwb_ref[:, lanes], (CONV_ROWS, CONV_LANES))
            for b in range(min(SUBLANES, width)):
                start, shift = divmod(base + b, SUBLANES)
                start = r + start * SUBLANES
                extra = SUBLANES if shift else 0
                part = None
                for k in range(b, width, SUBLANES):
                    rows = slice(start + k - b, start + k - b + CONV_ROWS + extra)
                    term = dw_ref[k:k + 1, lanes] * gbuf_ref[rows, lanes]
                    part = term if part is None else part + term
                y = y + part[shift:shift + CONV_ROWS]
            ybuf_ref[r:r + CONV_ROWS, lanes] = y
    gbuf_ref[0:halo, :] = gbuf_ref[tm:tm + halo, :]

    y = _layer_norm(ybuf_ref[...], ng_ref[...], nb_ref[...])
    y = (y * _sigmoid(y)).astype(jnp.bfloat16)
    mix = jnp.dot(y, wout_ref[...], preferred_element_type=jnp.float32)
    o_ref[...] = _layer_norm(ALPHA * x + mix, g_ref[...], beta_ref[...])


def _conformer(x, w_in, b_in, dw_w, dw_b, ng, nb, w_out, g, beta, tm=512):
    t, d = x.shape
    width = dw_w.shape[0]
    halo = -(-(width - 1) // SUBLANES) * SUBLANES
    consts = (w_in, b_in, dw_w, dw_b, ng, nb, w_out, g, beta)
    return pl.pallas_call(
        _conformer_kernel,
        grid=(t // tm,),
        in_specs=[pl.BlockSpec((tm, d), lambda i: (i, 0))] + [_const_spec(c.shape) for c in consts],
        out_specs=pl.BlockSpec((tm, d), lambda i: (i, 0)),
        out_shape=jax.ShapeDtypeStruct((t, d), jnp.float32),
        scratch_shapes=[pltpu.VMEM((tm + halo, d), jnp.float32), pltpu.VMEM((tm, d), jnp.float32)],
        compiler_params=_params(("arbitrary",)),
        name="conformer_conv_norm",
    )(x, *consts)


def _bias_kernel(tbl_ref, o_ref, *, kind):
    step = pl.program_id(0)
    rows = o_ref.shape[-2]
    out = o_ref if kind != "cmp" else o_ref.at[0]
    per_q = Q_BLOCK // CMP_STRIDE
    for sub in range(rows // SUB_ROWS):
        row = lax.broadcasted_iota(jnp.int32, (SUB_ROWS, Q_BLOCK), 0) + sub * SUB_ROWS
        lane = lax.broadcasted_iota(jnp.int32, (SUB_ROWS, Q_BLOCK), 1)
        if kind == "near":
            dist = lane + NEAR_BACK - (step * rows + row)
        elif kind == "win":
            dist = lane + WINDOW - (step * rows + row)
        else:
            c0 = jnp.maximum(step * per_q - (CMP_WIN - per_q), 0)
            dist = step * Q_BLOCK + lane - ((c0 + row) * CMP_STRIDE + CMP_BLOCK - 1)
        acc = [jnp.full((SUB_ROWS, Q_BLOCK), NEG, jnp.float32) for _ in range(B_HEADS)]
        for b in range(REL_BUCKETS):
            reached = dist >= BUCKET_START[b]
            for h in range(B_HEADS):
                bias = (tbl_ref[b, h] - tbl_ref[REL_BUCKETS - 1, h]) * LOG2E
                acc[h] = jnp.where(reached, bias, acc[h])
        for h in range(B_HEADS):
            if kind == "win":
                acc[h] = jnp.where(dist >= WINDOW, NEG, acc[h])
            out[sub * SUB_ROWS:(sub + 1) * SUB_ROWS, h * Q_BLOCK:(h + 1) * Q_BLOCK] = acc[h]


def _bias_table(rel_table, kind, shape, block):
    nd = len(shape)
    return pl.pallas_call(
        functools.partial(_bias_kernel, kind=kind),
        grid=(shape[0] // block[0],),
        in_specs=[pl.BlockSpec(memory_space=pltpu.SMEM)],
        out_specs=pl.BlockSpec(block, lambda i: (i,) + (0,) * (nd - 1)),
        out_shape=jax.ShapeDtypeStruct(shape, jnp.float32),
        compiler_params=_params(("parallel",)),
        name="rel_bias_" + kind,
    )(rel_table)


def _attention_tables(rel_table):
    bcmp = _bias_table(rel_table, "cmp", (CMP_SPECIAL, CMP_WIN, NL), (1, CMP_WIN, NL))
    bnear = _bias_table(rel_table, "near", (NEAR_ROWS, NL), (Q_BLOCK, NL))
    bwin = _bias_table(rel_table, "win", (WIN_ROWS, NL), (Q_BLOCK, NL))
    return bcmp, bnear, bwin


def _ab_weights(w_in, w_out):
    d = w_in.shape[0]
    o = 2 * A_WIDTH
    w_uv = w_in[:, :o]
    w_q = w_in[:, o:o + B_WIDTH].reshape(d, B_HEADS, B_HEAD_DIM)
    o += B_WIDTH
    q_pad = jnp.zeros((d, B_HEADS, B_KV_GROUPS, B_HEAD_DIM), w_in.dtype)
    for h in range(B_HEADS):
        q_pad = q_pad.at[:, h, h // B_HPG].set(w_q[:, h])
    w_kv = w_in[:, o:o + 6 * KV_WIDTH]
    o += 6 * KV_WIDTH
    w_g = w_in[:, o:].reshape(d, B_HEADS, 3).transpose(0, 2, 1).reshape(d, 3 * B_HEADS)
    w_g = jnp.pad(w_g, ((0, 0), (0, LANES - 3 * B_HEADS)))
    w = jnp.concatenate([w_uv, q_pad.reshape(d, B_HEADS * LANES), w_kv, w_g], axis=1)
    return w.astype(jnp.bfloat16), w_out[:A_WIDTH].astype(jnp.bfloat16), w_out[A_WIDTH:].astype(jnp.bfloat16)


def _compress_weights(pe_k, w1_k, w2_k, pe_v, w1_v, w2_v):
    hid = w1_k.shape[1]
    nseg = 2 * B_KV_GROUPS
    halves = []
    pes = []
    for half in range(CMP_BLOCK // CMP_STRIDE):
        rows = slice(half * CMP_STRIDE, (half + 1) * CMP_STRIDE)
        w = jnp.zeros((CMP_STRIDE, nseg, B_HEAD_DIM, nseg, hid), jnp.float32)
        pe = jnp.zeros((CMP_STRIDE, nseg, B_HEAD_DIM), jnp.float32)
        for seg in range(nseg):
            w1, pe_src = (w1_k, pe_k) if seg < B_KV_GROUPS else (w1_v, pe_v)
            w = w.at[:, seg, :, seg, :].set(w1.reshape(CMP_BLOCK, B_HEAD_DIM, hid)[rows])
            pe = pe.at[:, seg, :].set(pe_src[rows])
        halves.append(w.reshape(CMP_STRIDE * nseg * B_HEAD_DIM, nseg * hid).astype(jnp.bfloat16))
        pes.append(pe.reshape(1, CMP_STRIDE * nseg * B_HEAD_DIM))
    w2 = jnp.zeros((nseg, hid, nseg, B_HEAD_DIM), jnp.float32)
    for seg in range(nseg):
        w2 = w2.at[seg, :, seg, :].set(w2_k if seg < B_KV_GROUPS else w2_v)
    return pes[0], pes[1], halves[0], halves[1], w2.reshape(nseg * hid, nseg * B_HEAD_DIM).astype(jnp.bfloat16)


def _row(v):
    return v.reshape(1, -1)


def _mixer_ab_layer(x, rel_table, w_in, sgu_ln_g, sgu_ln_b, sgu_w, sgu_b,
                    pe_k, w1_k, w2_k, pe_v, w1_v, w2_v, w_out, ln_g, ln_b):
    t = x.shape[0]
    w_all, w_out_a, w_out_b = _ab_weights(w_in, w_out)
    uv, q, cmp_in, kv, gates = _proj(x, w_all)

    causal = jnp.tril(jnp.ones((A_CHUNK, A_CHUNK), dtype=bool))
    ws = jnp.where(causal[None], sgu_w, 0.0).astype(jnp.bfloat16)
    sgu_bias = jnp.repeat(sgu_b.T, A_GROUP_DIM, axis=1)
    a_out = _sgu(uv, _row(sgu_ln_g), _row(sgu_ln_b), ws, sgu_bias)

    pea, peb, wa, wb, w2 = _compress_weights(pe_k, w1_k, w2_k, pe_v, w1_v, w2_v)
    cmp_out = _compress(cmp_in.reshape(t // CMP_STRIDE, CMP_STRIDE * 2 * KV_WIDTH), pea, peb, wa, wb, w2)
    kc = cmp_out[:, :KV_WIDTH].astype(jnp.bfloat16)
    vct = cmp_out[:, KV_WIDTH:].T.astype(jnp.bfloat16)

    ks, vs, kw, vw = (kv[:, n * KV_WIDTH:(n + 1) * KV_WIDTH] for n in range(4))
    kw_pad = jnp.pad(kw, ((WINDOW, 0), (0, 0)))
    vwt_pad = jnp.pad(vw, ((WINDOW, 0), (0, 0))).T
    bcmp, bnear, bwin = _attention_tables(rel_table)
    b_out = _nsa(q, gates, kc, vct, ks, vs.T, kw_pad, vwt_pad, bcmp, bnear, bwin)

    return _out_proj(a_out, b_out, x, w_out_a, w_out_b, _row(ln_g), _row(ln_b))


def kernel(x, rel_table, ab_w_in, ab_sgu_ln_g, ab_sgu_ln_b, ab_sgu_w, ab_sgu_b, ab_cmp_pe_k, ab_cmp_w1_k, ab_cmp_w2_k, ab_cmp_pe_v, ab_cmp_w1_v, ab_cmp_w2_v, ab_w_out, c_w_in, c_b_in, c_dw_w, c_dw_b, c_norm_g, c_norm_b, c_w_out, ffn_w_up, ffn_conv_w, ffn_conv_b, ffn_w_down, ln_mix_g, ln_mix_b, ln_ffn_g, ln_ffn_b):
    bsz = x.shape[0]
    outs = []
    for bi in range(bsz):
        h = x[bi]
        for layer in range(DEPTH):
            i = layer // 2
            if layer % 2 == 0:
                h = _mixer_ab_layer(h, rel_table, ab_w_in[i], ab_sgu_ln_g[i], ab_sgu_ln_b[i], ab_sgu_w[i],
                                    ab_sgu_b[i], ab_cmp_pe_k[i], ab_cmp_w1_k[i], ab_cmp_w2_k[i],
                                    ab_cmp_pe_v[i], ab_cmp_w1_v[i], ab_cmp_w2_v[i], ab_w_out[i],
                                    ln_mix_g[layer], ln_mix_b[layer])
            else:
                h = _conformer(h, c_w_in[i].astype(jnp.bfloat16), _row(c_b_in[i]), c_dw_w[i], _row(c_dw_b[i]),
                               _row(c_norm_g[i]), _row(c_norm_b[i]), c_w_out[i].astype(jnp.bfloat16),
                               _row(ln_mix_g[layer]), _row(ln_mix_b[layer]))
            h = _conv_ffn(h, ffn_w_up[layer].astype(jnp.bfloat16), ffn_conv_w[layer], _row(ffn_conv_b[layer]),
                          ffn_w_down[layer].astype(jnp.bfloat16), _row(ln_ffn_g[layer]), _row(ln_ffn_b[layer]))
        outs.append(h)
    return jnp.stack(outs)
```
